```python
import math
import jax, jax.numpy as jnp
from jax import lax
import numpy as np

D_MODEL = 4096
BATCH = 4
SEQ = 2048
DEPTH = 4
DEC_BATCH = 8
DEC_SEQ = 4
PAST_LEN = 8192
PAGE_SIZE = 128

HEAD_DIM = 128
C_CONV = D_MODEL // 2
CONV_W = 31
H_DIFF = D_MODEL // (4 * HEAD_DIM)
DIFF_W = H_DIFF * 2 * HEAD_DIM
H_NSA = D_MODEL // (2 * HEAD_DIM)
G_NSA = 4
HG_NSA = H_NSA // G_NSA
NSA_QW = H_NSA * HEAD_DIM
NSA_KVW = G_NSA * HEAD_DIM
NSA_BLOCK = 64
NSA_TOPK = 16
WINDOW = 512
D_FF = ((8 * D_MODEL + 3 * 256 - 1) // (3 * 256)) * 256
N_IN = 2 * C_CONV + 3 * DIFF_W + NSA_QW + 6 * NSA_KVW + 3 * H_NSA + 3 * D_MODEL
Q_BLOCK = 128
NSA_Q_CHUNK = 16
NEG = -1e30
FORCE = 1e9

kernel_name = 'hybrid_conv_diffattn_nsa_step'


def _rms_norm(x, g, eps=1e-6):
    xf = x.astype(jnp.float32)
    y = xf * lax.rsqrt(jnp.mean(xf * xf, axis=-1, keepdims=True) + eps)
    return (y * g.astype(jnp.float32)).astype(x.dtype)


def _layer_norm(x, g, b, eps=1e-5):
    xf = x.astype(jnp.float32)
    xc = xf - jnp.mean(xf, axis=-1, keepdims=True)
    y = xc * lax.rsqrt(jnp.mean(xc * xc, axis=-1, keepdims=True) + eps)
    return (y * g.astype(jnp.float32) + b.astype(jnp.float32)).astype(x.dtype)


def _gather_pages(cache_l, page_table):
    g = cache_l[page_table]
    return g.reshape((g.shape[0], g.shape[1] * g.shape[2]) + g.shape[3:])


def _conv_module(u, buf, w, b, ln_g, ln_b):
    xin = jnp.concatenate([buf, u], axis=1)
    y = lax.conv_general_dilated(xin, w[:, None, :], (1,), 'VALID',
                                 dimension_numbers=('NWC', 'WIO', 'NWC'),
                                 feature_group_count=C_CONV)
    y = jax.nn.silu(_layer_norm(y + b, ln_g, ln_b))
    return y, xin[:, xin.shape[1] - (CONV_W - 1):]


def _diff_attend(q, k, v, t0, lam):
    B, T = q.shape[:2]
    L = k.shape[1]
    qb = math.gcd(T, Q_BLOCK)
    nb = T // qb
    qs = q.reshape((B, nb, qb) + q.shape[2:]).swapaxes(0, 1)
    starts = jnp.arange(nb) * qb
    kpos = jnp.arange(L)
    scale = HEAD_DIM ** -0.5

    def block(args):
        qblk, s0 = args
        tpos = t0 + s0 + jnp.arange(qb)
        s = jnp.einsum('bqhcd,bkhcd->bhcqk', qblk, k).astype(jnp.float32) * scale
        p = jax.nn.softmax(jnp.where(kpos[None, :] <= tpos[:, None], s, NEG), axis=-1)
        a = p[:, :, 0] - lam * p[:, :, 1]
        return jnp.einsum('bhqk,bkhe->bqhe', a.astype(v.dtype), v)

    o = lax.map(block, (qs, starts))
    return o.swapaxes(0, 1).reshape((B, T) + o.shape[3:])


def _nsa_attend(q, gates, kc, vc, ks, vs, kw, vw, t0, wc_k, wc_v):
    B, T = q.shape[:2]
    L = kc.shape[1]
    nblk = -(-L // NSA_BLOCK)
    pad = nblk * NSA_BLOCK - L

    def blocks(a):
        a = jnp.pad(a, ((0, 0), (0, pad), (0, 0), (0, 0)))
        return a.reshape(B, nblk, NSA_BLOCK, G_NSA, HEAD_DIM)

    kcb = jnp.einsum('bnjgd,jgd->bngd', blocks(kc), wc_k)
    vcb = jnp.einsum('bnjgd,jgd->bngd', blocks(vc), wc_v)
    ksg = blocks(ks).transpose(0, 3, 1, 2, 4)
    vsg = blocks(vs).transpose(0, 3, 1, 2, 4)
    topk = min(NSA_TOPK, nblk)
    qc = math.gcd(T, NSA_Q_CHUNK)
    nch = T // qc
    qs = q.reshape(B, nch, qc, G_NSA, HG_NSA, HEAD_DIM).swapaxes(0, 1)
    gs = gates.reshape(B, nch, qc, G_NSA, HG_NSA, 3).swapaxes(0, 1)
    starts = jnp.arange(nch) * qc
    blk = jnp.arange(nblk)
    bi = jnp.arange(B)[:, None, None, None]
    gi = jnp.arange(G_NSA)[None, None, :, None]
    scale = HEAD_DIM ** -0.5

    def chunk(args):
        qb, gb, s0 = args
        tpos = t0 + s0 + jnp.arange(qc)
        complete = (blk[None, :] + 1) * NSA_BLOCK <= tpos[:, None] + 1
        cm = complete[None, :, None, None, :]
        s_c = jnp.einsum('bqghd,bngd->bqghn', qb, kcb).astype(jnp.float32) * scale
        p_c = jax.nn.softmax(jnp.where(cm, s_c, NEG), axis=-1) * cm
        o_c = jnp.einsum('bqghn,bngd->bqghd', p_c.astype(vcb.dtype), vcb)
        cur = blk[None, :] == tpos[:, None] // NSA_BLOCK
        imp = jnp.sum(p_c, axis=3)
        score = jnp.where(cur[None, :, None, :], FORCE,
                          jnp.where(complete[None, :, None, :], imp, -1.0))
        top_s, idx = lax.top_k(score, topk)
        k_sel = ksg[bi, gi, idx]
        v_sel = vsg[bi, gi, idx]
        kpos = idx[..., None] * NSA_BLOCK + jnp.arange(NSA_BLOCK)
        sm = (top_s[..., None] >= 0) & (kpos <= tpos[None, :, None, None, None])
        s_s = jnp.einsum('bqghd,bqgkjd->bqghkj', qb, k_sel).astype(jnp.float32) * scale
        p_s = jax.nn.softmax(jnp.where(sm[:, :, :, None], s_s, NEG), axis=(-2, -1))
        o_s = jnp.einsum('bqghkj,bqgkjd->bqghd', p_s.astype(v_sel.dtype), v_sel)
        kwb = lax.dynamic_slice_in_dim(kw, s0, WINDOW + qc, axis=1)
        vwb = lax.dynamic_slice_in_dim(vw, s0, WINDOW + qc, axis=1)
        wpos = t0 - WINDOW + s0 + jnp.arange(WINDOW + qc)
        wm = ((wpos[None, :] <= tpos[:, None]) & (wpos[None, :] > tpos[:, None] - WINDOW)
              & (wpos[None, :] >= 0))
        s_w = jnp.einsum('bqghd,bkgd->bqghk', qb, kwb).astype(jnp.float32) * scale
        p_w = jax.nn.softmax(jnp.where(wm[None, :, None, None, :], s_w, NEG), axis=-1)
        o_w = jnp.einsum('bqghk,bkgd->bqghd', p_w.astype(vwb.dtype), vwb)
        return gb[..., 0:1] * o_c + gb[..., 1:2] * o_s + gb[..., 2:3] * o_w

    o = lax.map(chunk, (qs, gs, starts))
    return o.swapaxes(0, 1).reshape(B, T, NSA_QW)


def _window_rows(a, buf):
    B, T = a.shape[:2]
    if buf is None:
        rows = jnp.concatenate([jnp.zeros((B, WINDOW) + a.shape[2:], a.dtype), a], axis=1)
        keep = min(WINDOW, T)
        return rows, a[:, T - keep:]
    wb = buf.shape[1]
    cat = jnp.concatenate([buf, a], axis=1)
    rows = jnp.pad(cat, ((0, 0), (WINDOW - wb, 0), (0, 0), (0, 0)))
    return rows, cat[:, T:]


def _mixer(h, l, W, past):
    B, T, _ = h.shape
    t0 = 0 if past is None else past['diff_k'].shape[1]
    proj = h @ W['w_in']
    sizes = [C_CONV, C_CONV, DIFF_W, DIFF_W, DIFF_W, NSA_QW] + [NSA_KVW] * 6 + [3 * H_NSA, D_MODEL, D_MODEL, D_MODEL]
    (glu_a, glu_b, dq, dk, dv, nq, ck, cv, sk, sv, wk, wv, ng, mg1, mg2, mg3) = jnp.split(
        proj, np.cumsum(sizes)[:-1].tolist(), axis=-1)

    u = glu_a * jax.nn.sigmoid(glu_b)
    buf = jnp.zeros((B, CONV_W - 1, C_CONV), u.dtype) if past is None else past['conv']
    y_conv, conv_state = _conv_module(u, buf, W['conv_w'], W['conv_b'], W['conv_ln_g'], W['conv_ln_b'])
    y_conv = y_conv @ W['w_conv_out']

    dk = dk.reshape(B, T, H_DIFF, 2 * HEAD_DIM)
    dv = dv.reshape(B, T, H_DIFF, 2 * HEAD_DIM)
    k_all = dk if past is None else jnp.concatenate([past['diff_k'], dk], axis=1)
    v_all = dv if past is None else jnp.concatenate([past['diff_v'], dv], axis=1)
    lam_init = 0.8 - 0.6 * math.exp(-0.3 * l)
    lam = (jnp.exp(jnp.sum((W['lam_q1'] * W['lam_k1']).astype(jnp.float32)))
           - jnp.exp(jnp.sum((W['lam_q2'] * W['lam_k2']).astype(jnp.float32))) + lam_init)
    o_d = _diff_attend(dq.reshape(B, T, H_DIFF, 2, HEAD_DIM),
                       k_all.reshape(B, -1, H_DIFF, 2, HEAD_DIM), v_all, t0, lam)
    o_d = _rms_norm(o_d, W['g_subln']) * (1.0 - lam_init)
    y_diff = o_d.reshape(B, T, DIFF_W) @ W['w_diff_out']

    r = lambda a: a.reshape(B, T, G_NSA, HEAD_DIM)
    ck, cv, sk, sv, wk, wv = r(ck), r(cv), r(sk), r(sv), r(wk), r(wv)
    cat = lambda name, a: a if past is None else jnp.concatenate([past[name], a], axis=1)
    kw_rows, new_wk = _window_rows(wk, None if past is None else past['win_k'])
    vw_rows, new_wv = _window_rows(wv, None if past is None else past['win_v'])
    gates = jax.nn.sigmoid(ng.reshape(B, T, H_NSA, 3))
    o_n = _nsa_attend(nq.reshape(B, T, H_NSA, HEAD_DIM), gates,
                      cat('cmp_k', ck), cat('cmp_v', cv), cat('sel_k', sk), cat('sel_v', sv),
                      kw_rows, vw_rows, t0, W['w_cmp_k'], W['w_cmp_v'])
    y_nsa = o_n @ W['w_nsa_out']

    merged = jax.nn.sigmoid(mg1) * y_conv + jax.nn.sigmoid(mg2) * y_diff + jax.nn.sigmoid(mg3) * y_nsa
    out = merged @ W['w_out']
    state = {'diff_k': dk, 'diff_v': dv, 'cmp_k': ck, 'cmp_v': cv, 'sel_k': sk, 'sel_v': sv,
             'win_k': new_wk, 'win_v': new_wv, 'conv': conv_state}
    return out, state


def _layer(x, c, l, W, past):
    mod = jax.nn.silu(c) @ W['w_ada'] + W['b_ada']
    sh1, sc1, gt1, sh2, sc2, gt2 = jnp.split(mod[:, None, :], 6, axis=-1)
    h = _rms_norm(x, W['g_pre_mix']) * (1 + sc1) + sh1
    y, state = _mixer(h, l, W, past)
    x = x + gt1 * _rms_norm(y, W['g_post_mix'])
    h = _rms_norm(x, W['g_pre_ffn']) * (1 + sc2) + sh2
    g, u = jnp.split(h @ W['w_gate_up'], 2, axis=-1)
    f = (jax.nn.silu(g) * u) @ W['w_down']
    x = x + gt2 * _rms_norm(f, W['g_post_ffn'])
    return x, state


def setup_inputs(seed: int = 0) -> dict:
    key = jax.random.key(seed)
    keys = iter(jax.random.split(key, 48))

    def nrm(shape, s):
        return jax.random.normal(next(keys), shape, jnp.float32) * s

    def gain(shape):
        return 1.0 + nrm(shape, 0.02)

    D = D_MODEL
    n_pages = PAST_LEN // PAGE_SIZE
    n_used = DEC_BATCH * n_pages
    n_pool = n_used + max(1, n_used // 4)
    wb = min(WINDOW, PAST_LEN)
    page_table = jax.random.permutation(next(keys), n_pool)[:n_used].reshape(DEC_BATCH, n_pages).astype(jnp.int32)
    inputs = {
        'x_prompt': nrm((BATCH, SEQ, D), 1.0),
        'x_sample': nrm((DEC_BATCH, DEC_SEQ, D), 1.0),
        'cache_diff_k': nrm((DEPTH, n_pool, PAGE_SIZE, H_DIFF, 2 * HEAD_DIM), 1.0),
        'cache_diff_v': nrm((DEPTH, n_pool, PAGE_SIZE, H_DIFF, 2 * HEAD_DIM), 1.0),
        'cache_cmp_k': nrm((DEPTH, n_pool, PAGE_SIZE, G_NSA, HEAD_DIM), 1.0),
        'cache_cmp_v': nrm((DEPTH, n_pool, PAGE_SIZE, G_NSA, HEAD_DIM), 1.0),
        'cache_sel_k': nrm((DEPTH, n_pool, PAGE_SIZE, G_NSA, HEAD_DIM), 1.0),
        'cache_sel_v': nrm((DEPTH, n_pool, PAGE_SIZE, G_NSA, HEAD_DIM), 1.0),
        'state_win_k': nrm((DEPTH, DEC_BATCH, wb, G_NSA, HEAD_DIM), 1.0),
        'state_win_v': nrm((DEPTH, DEC_BATCH, wb, G_NSA, HEAD_DIM), 1.0),
        'state_conv': nrm((DEPTH, DEC_BATCH, CONV_W - 1, C_CONV), 0.5),
        'page_table': page_table,
        'c_prompt': nrm((BATCH, D), 1.0),
        'c_sample': nrm((DEC_BATCH, D), 1.0),
        'w_ada': nrm((DEPTH, D, 6 * D), 0.5 * D ** -0.5),
        'b_ada': nrm((DEPTH, 6 * D), 0.02),
        'g_pre_mix': gain((DEPTH, D)),
        'g_post_mix': gain((DEPTH, D)),
        'g_pre_ffn': gain((DEPTH, D)),
        'g_post_ffn': gain((DEPTH, D)),
        'w_in': nrm((DEPTH, D, N_IN), D ** -0.5),
        'conv_w': nrm((DEPTH, CONV_W, C_CONV), CONV_W ** -0.5),
        'conv_b': nrm((DEPTH, C_CONV), 0.02),
        'conv_ln_g': gain((DEPTH, C_CONV)),
        'conv_ln_b': nrm((DEPTH, C_CONV), 0.02),
        'w_conv_out': nrm((DEPTH, C_CONV, D), C_CONV ** -0.5),
        'lam_q1': nrm((DEPTH, HEAD_DIM), 0.1),
        'lam_k1': nrm((DEPTH, HEAD_DIM), 0.1),
        'lam_q2': nrm((DEPTH, HEAD_DIM), 0.1),
        'lam_k2': nrm((DEPTH, HEAD_DIM), 0.1),
        'g_subln': gain((DEPTH, 2 * HEAD_DIM)),
        'w_diff_out': nrm((DEPTH, DIFF_W, D), DIFF_W ** -0.5),
        'w_cmp_k': (1.0 + nrm((DEPTH, NSA_BLOCK, G_NSA, HEAD_DIM), 0.1)) / NSA_BLOCK,
        'w_cmp_v': (1.0 + nrm((DEPTH, NSA_BLOCK, G_NSA, HEAD_DIM), 0.1)) / NSA_BLOCK,
        'w_nsa_out': nrm((DEPTH, NSA_QW, D), NSA_QW ** -0.5),
        'w_out': nrm((DEPTH, D, D), D ** -0.5),
        'w_gate_up': nrm((DEPTH, D, 2 * D_FF), D ** -0.5),
        'w_down': nrm((DEPTH, D_FF, D), D_FF ** -0.5),
    }
    return inputs


def reference(x_prompt, x_sample, cache_diff_k, cache_diff_v, cache_cmp_k, cache_cmp_v, cache_sel_k, cache_sel_v,
              state_win_k, state_win_v, state_conv, page_table, c_prompt, c_sample,
              w_ada, b_ada, g_pre_mix, g_post_mix, g_pre_ffn, g_post_ffn, w_in, conv_w, conv_b, conv_ln_g, conv_ln_b,
              w_conv_out, lam_q1, lam_k1, lam_q2, lam_k2, g_subln, w_diff_out, w_cmp_k, w_cmp_v, w_nsa_out,
              w_out, w_gate_up, w_down):
    names = ('diff_k', 'diff_v', 'cmp_k', 'cmp_v', 'sel_k', 'sel_v', 'win_k', 'win_v', 'conv')
    pst = {n: [] for n in names}
    sst = {n: [] for n in names}
    yp, ys = x_prompt, x_sample
    for l in range(DEPTH):
        W = {'w_ada': w_ada[l], 'b_ada': b_ada[l], 'g_pre_mix': g_pre_mix[l], 'g_post_mix': g_post_mix[l],
             'g_pre_ffn': g_pre_ffn[l], 'g_post_ffn': g_post_ffn[l], 'w_in': w_in[l], 'conv_w': conv_w[l],
             'conv_b': conv_b[l], 'conv_ln_g': conv_ln_g[l], 'conv_ln_b': conv_ln_b[l], 'w_conv_out': w_conv_out[l],
             'lam_q1': lam_q1[l], 'lam_k1': lam_k1[l], 'lam_q2': lam_q2[l], 'lam_k2': lam_k2[l],
             'g_subln': g_subln[l], 'w_diff_out': w_diff_out[l], 'w_cmp_k': w_cmp_k[l], 'w_cmp_v': w_cmp_v[l],
             'w_nsa_out': w_nsa_out[l], 'w_out': w_out[l], 'w_gate_up': w_gate_up[l], 'w_down': w_down[l]}
        yp, st = _layer(yp, c_prompt, l, W, None)
        for n in names:
            pst[n].append(st[n])
        past = {'diff_k': _gather_pages(cache_diff_k[l], page_table),
                'diff_v': _gather_pages(cache_diff_v[l], page_table),
                'cmp_k': _gather_pages(cache_cmp_k[l], page_table),
                'cmp_v': _gather_pages(cache_cmp_v[l], page_table),
                'sel_k': _gather_pages(cache_sel_k[l], page_table),
                'sel_v': _gather_pages(cache_sel_v[l], page_table),
                'win_k': state_win_k[l], 'win_v': state_win_v[l], 'conv': state_conv[l]}
        ys, st = _layer(ys, c_sample, l, W, past)
        for n in names:
            sst[n].append(st[n])
    p_diff_k = jnp.stack(pst['diff_k'])
    p_diff_v = jnp.stack(pst['diff_v'])
    p_cmp_k = jnp.stack(pst['cmp_k'])
    p_cmp_v = jnp.stack(pst['cmp_v'])
    p_sel_k = jnp.stack(pst['sel_k'])
    p_sel_v = jnp.stack(pst['sel_v'])
    p_win_k = jnp.stack(pst['win_k'])
    p_win_v = jnp.stack(pst['win_v'])
    p_conv = jnp.stack(pst['conv'])
    s_diff_k = jnp.stack(sst['diff_k'])
    s_diff_v = jnp.stack(sst['diff_v'])
    s_cmp_k = jnp.stack(sst['cmp_k'])
    s_cmp_v = jnp.stack(sst['cmp_v'])
    s_sel_k = jnp.stack(sst['sel_k'])
    s_sel_v = jnp.stack(sst['sel_v'])
    s_win_k = jnp.stack(sst['win_k'])
    s_win_v = jnp.stack(sst['win_v'])
    s_conv = jnp.stack(sst['conv'])
    return (yp, ys, p_diff_k, p_diff_v, p_cmp_k, p_cmp_v, p_sel_k, p_sel_v, p_win_k, p_win_v, p_conv,
            s_diff_k, s_diff_v, s_cmp_k, s_cmp_v, s_sel_k, s_sel_v, s_win_k, s_win_v, s_conv)
```

```python
import functools
import math

import jax
import jax.numpy as jnp
from jax import lax
from jax.experimental import pallas as pl
from jax.experimental.pallas import tpu as pltpu

F32 = jnp.float32
BF16 = jnp.bfloat16

HEAD_DIM = 128
G_NSA = 4
NSA_BLOCK = 64
NSA_TOPK = 16
WINDOW = 512
NEG = -1e30
FORCE = 1e9
LANE = 128
V7X_VMEM_LIMIT = 56 * 1024 * 1024
NT_DIMS = (((1,), (1,)), ((), ()))


def _cp(*sem):
    return pltpu.CompilerParams(dimension_semantics=sem, vmem_limit_bytes=V7X_VMEM_LIMIT)


def _pick(n, cands):
    for c in cands:
        if n % c == 0:
            return c
    return n


def _sigmoid(x):
    return 1.0 / (1.0 + jnp.exp(-x))


def _vdiv(x, n):
    assert n & (n - 1) == 0
    return lax.shift_right_logical(x, jnp.int32(n.bit_length() - 1))


def _vmod(x, n):
    assert n & (n - 1) == 0
    return lax.bitwise_and(x, jnp.int32(n - 1))


def _iota(shape, dim):
    return lax.broadcasted_iota(jnp.int32, shape, dim)


def _mm_kernel(*refs, n_w, epilogue, cast_w, cast_rows):
    x_ref = refs[0]
    w_refs = refs[1:1 + n_w]
    o_ref = refs[1 + n_w]
    if cast_w:
        wb_refs = refs[2 + n_w:]
        k_dim = w_refs[0].shape[0]

        @pl.when(pl.program_id(1) == 0)
        def _():
            def body(c, carry):
                r0 = pl.multiple_of(c * cast_rows, cast_rows)
                for w, wb in zip(w_refs, wb_refs):
                    wb[pl.ds(r0, cast_rows), :] = w[pl.ds(r0, cast_rows), :].astype(BF16)
                return carry
            lax.fori_loop(0, k_dim // cast_rows, body, 0)
        ws = wb_refs
    else:
        ws = w_refs
    x = x_ref[...]
    accs = [jnp.dot(x, w[...], preferred_element_type=F32) for w in ws]
    if epilogue == "none":
        r = accs[0]
    elif epilogue == "sigmoid":
        r = _sigmoid(accs[0])
    elif epilogue == "glu":
        r = accs[0] * _sigmoid(accs[1])
    else:
        r = accs[0] * _sigmoid(accs[0]) * accs[1]
    o_ref[...] = r.astype(o_ref.dtype)


def _mm(x, w, l, col_off, n_cols, *, out_dtype, epilogue="none", pair_off=None, tn=512):
    M, K = x.shape
    n_w = 1 if pair_off is None else 2
    cast_w = w.dtype != BF16
    tm = _pick(M, (512,)) if (K > 4096 or n_w == 2 and tn > 256) else _pick(M, (1024, 512, 256))
    assert n_cols % tn == 0 and col_off % tn == 0 and (pair_off or 0) % tn == 0
    offs = [col_off // tn] if n_w == 1 else [col_off // tn, (col_off + pair_off) // tn]
    in_specs = [pl.BlockSpec((tm, K), lambda j, i: (i, 0))]
    for o in offs:
        in_specs.append(pl.BlockSpec((None, K, tn), lambda j, i, o=o: (l, 0, o + j)))
    scratch = [pltpu.VMEM((K, tn), BF16) for _ in range(n_w)] if cast_w else []
    return pl.pallas_call(
        functools.partial(_mm_kernel, n_w=n_w, epilogue=epilogue, cast_w=cast_w,
                          cast_rows=_pick(K, (256, 128, 64, 16))),
        grid=(n_cols // tn, M // tm),
        in_specs=in_specs,
        out_specs=pl.BlockSpec((tm, tn), lambda j, i: (i, j)),
        out_shape=jax.ShapeDtypeStruct((M, n_cols), out_dtype),
        scratch_shapes=scratch,
        compiler_params=_cp("arbitrary", "arbitrary"),
    )(x, *([w] * n_w))


def _cast_kernel(x_ref, o_ref):
    o_ref[...] = x_ref[...].astype(o_ref.dtype)


def _cast_bf16(w):
    dep, K, N = w.shape
    tk = _pick(K, (512, 256, 128, 64, 16))
    return pl.pallas_call(
        _cast_kernel,
        grid=(dep, K // tk),
        in_specs=[pl.BlockSpec((None, tk, N), lambda d, k: (d, k, 0))],
        out_specs=pl.BlockSpec((None, tk, N), lambda d, k: (d, k, 0)),
        out_shape=jax.ShapeDtypeStruct(w.shape, BF16),
        compiler_params=_cp("arbitrary", "arbitrary"),
    )(w)


def _merge_kernel(a1, a2, a3, w1, w2, w3, g1, g2, g3, o_ref, wb1, wb2, wb3, *, cast_rows):
    k_dim = w1.shape[0]

    @pl.when(pl.program_id(1) == 0)
    def _():
        def body(c, carry):
            r0 = pl.multiple_of(c * cast_rows, cast_rows)
            for w, wb in ((w1, wb1), (w2, wb2), (w3, wb3)):
                wb[pl.ds(r0, cast_rows), :] = w[pl.ds(r0, cast_rows), :].astype(BF16)
            return carry
        lax.fori_loop(0, k_dim // cast_rows, body, 0)

    acc = g1[...].astype(F32) * jnp.dot(a1[...], wb1[...], preferred_element_type=F32)
    acc = acc + g2[...].astype(F32) * jnp.dot(a2[...], wb2[...], preferred_element_type=F32)
    acc = acc + g3[...].astype(F32) * jnp.dot(a3[...], wb3[...], preferred_element_type=F32)
    o_ref[...] = acc.astype(o_ref.dtype)


def _merge_mm(a_conv, a_diff, a_nsa, w_conv_out, w_diff_out, w_nsa_out, gates, l):
    M, K = a_conv.shape
    D = w_conv_out.shape[2]
    tm = _pick(M, (512, 256))
    tn = 512
    nb = D // tn
    a_spec = pl.BlockSpec((tm, K), lambda j, i: (i, 0))
    w_spec = pl.BlockSpec((None, K, tn), lambda j, i: (l, 0, j))
    g_specs = [pl.BlockSpec((tm, tn), lambda j, i, s=s: (i, s * nb + j)) for s in range(3)]
    return pl.pallas_call(
        functools.partial(_merge_kernel, cast_rows=_pick(K, (256, 128))),
        grid=(nb, M // tm),
        in_specs=[a_spec] * 3 + [w_spec] * 3 + g_specs,
        out_specs=pl.BlockSpec((tm, tn), lambda j, i: (i, j)),
        out_shape=jax.ShapeDtypeStruct((M, D), BF16),
        scratch_shapes=[pltpu.VMEM((K, tn), BF16) for _ in range(3)],
        compiler_params=_cp("arbitrary", "arbitrary"),
    )(a_conv, a_diff, a_nsa, w_conv_out, w_diff_out, w_nsa_out, gates, gates, gates)


def _ada_kernel(c_ref, w_ref, b_ref, o_ref):
    c = c_ref[...]
    a = (c * _sigmoid(c)).astype(BF16)
    o_ref[...] = jnp.dot(a, w_ref[...].astype(BF16), preferred_element_type=F32) + b_ref[...]


def _ada(c_all, w_ada, b_ada):
    dep, D, N = w_ada.shape
    R = c_all.shape[0]
    tn = 512
    return pl.pallas_call(
        _ada_kernel,
        grid=(dep, N // tn),
        in_specs=[pl.BlockSpec((R, D), lambda d, j: (0, 0)),
                  pl.BlockSpec((None, D, tn), lambda d, j: (d, 0, j)),
                  pl.BlockSpec((None, 1, tn), lambda d, j: (d, 0, j))],
        out_specs=pl.BlockSpec((None, R, tn), lambda d, j: (d, 0, j)),
        out_shape=jax.ShapeDtypeStruct((dep, R, N), F32),
        compiler_params=_cp("arbitrary", "arbitrary"),
    )(c_all, w_ada, b_ada.reshape(dep, 1, N))


def _norm_kernel(*refs, has_resid, has_h):
    it = iter(refs)
    x = next(it)[...]
    if has_resid:
        y = next(it)[...]
        g_post = next(it)[...]
        gate = next(it)[...]
    if has_h:
        g_pre = next(it)[...]
        sc = next(it)[...]
        sh = next(it)[...]
    if has_resid:
        yn = y * lax.rsqrt(jnp.mean(y * y, axis=-1, keepdims=True) + 1e-6) * g_post
        x = x + gate * yn
        next(it)[...] = x
    if has_h:
        hn = x * lax.rsqrt(jnp.mean(x * x, axis=-1, keepdims=True) + 1e-6) * g_pre
        next(it)[...] = (hn * (1.0 + sc) + sh).astype(BF16)


def _norm(x, mod, *, y=None, g_post=None, l_post=None, gate_chunk=None,
          g_pre=None, l_pre=None, sc_chunk=None, sh_chunk=None):
    S, T, D = x.shape
    R = mod.shape[2]
    tr = _pick(T, (256, 128, 64, 32, 16, 8))
    mr = 1 if R == 1 else tr
    has_resid = y is not None
    has_h = g_pre is not None
    row = pl.BlockSpec((None, tr, D), lambda s, i: (s, i, 0))

    def gspec(l):
        return pl.BlockSpec((None, 1, D), lambda s, i: (l, 0, 0))

    def mspec(l, k):
        if R == 1:
            return pl.BlockSpec((None, None, 1, D), lambda s, i: (l, s, 0, k))
        return pl.BlockSpec((None, None, mr, D), lambda s, i: (l, s, i, k))

    args, specs = [x], [row]
    if has_resid:
        args += [y, g_post.reshape(-1, 1, D), mod]
        specs += [row, gspec(l_post), mspec(l_post, gate_chunk)]
    if has_h:
        args += [g_pre.reshape(-1, 1, D), mod, mod]
        specs += [gspec(l_pre), mspec(l_pre, sc_chunk), mspec(l_pre, sh_chunk)]
    out_shape, out_specs = [], []
    if has_resid:
        out_shape.append(jax.ShapeDtypeStruct((S, T, D), F32))
        out_specs.append(row)
    if has_h:
        out_shape.append(jax.ShapeDtypeStruct((S, T, D), BF16))
        out_specs.append(row)
    outs = pl.pallas_call(
        functools.partial(_norm_kernel, has_resid=has_resid, has_h=has_h),
        grid=(S, T // tr),
        in_specs=specs, out_specs=out_specs, out_shape=out_shape,
        compiler_params=_cp("arbitrary", "arbitrary"),
    )(*args)
    return outs


HIST_ROWS = 32


def _conv_kernel(hist_ref, cur_ref, w_ref, b_ref, g_ref, be_ref, o_ref, xin_ref, y_ref, *,
                 tt, width, zero_first_hist):
    C = cur_ref.shape[-1]
    hist = hist_ref[...]
    if zero_first_hist:
        hist = jnp.where(pl.program_id(1) > 0, hist, 0.0)
    xin_ref[0:HIST_ROWS, :] = hist
    xin_ref[HIST_ROWS:HIST_ROWS + tt, :] = cur_ref[...]
    cc = _pick(C, (512, 256, 128))
    rc = min(64, tt)
    off = HIST_ROWS - (width - 1)

    def cbody(c, carry):
        c0 = pl.multiple_of(c * cc, cc)
        for r0 in range(0, tt, rc):
            acc = jnp.zeros((rc, cc), F32)
            for j in range(width):
                acc = acc + xin_ref[pl.ds(off + r0 + j, rc), pl.ds(c0, cc)] * w_ref[pl.ds(j, 1), pl.ds(c0, cc)]
            y_ref[pl.ds(r0, rc), pl.ds(c0, cc)] = acc
        return carry
    lax.fori_loop(0, C // cc, cbody, 0)

    rr = min(32, tt)

    def rbody(r, carry):
        r0 = pl.multiple_of(r * rr, rr)
        y = y_ref[pl.ds(r0, rr), :] + b_ref[...]
        mu = jnp.mean(y, axis=-1, keepdims=True)
        yc = y - mu
        var = jnp.mean(yc * yc, axis=-1, keepdims=True)
        yn = yc * lax.rsqrt(var + 1e-5) * g_ref[...] + be_ref[...]
        o_ref[pl.ds(r0, rr), :] = (yn * _sigmoid(yn)).astype(o_ref.dtype)
        return carry
    lax.fori_loop(0, tt // rr, rbody, 0)


def _conv_module(u, hist, conv_w, conv_b, ln_g, ln_b, l):
    S, T, C = u.shape
    width = conv_w.shape[1]
    tt = _pick(T, (256, 128, 64, 32, 16))
    if hist is None:
        hist_arr = u
        hist_spec = pl.BlockSpec((None, HIST_ROWS, C),
                                 lambda s, i: (s, jnp.maximum(i * (tt // HIST_ROWS) - 1, 0), 0))
    else:
        hist_arr = hist
        hist_spec = pl.BlockSpec((None, HIST_ROWS, C), lambda s, i: (s, 0, 0))
    vec = pl.BlockSpec((None, 1, C), lambda s, i: (l, 0, 0))
    return pl.pallas_call(
        functools.partial(_conv_kernel, tt=tt, width=width, zero_first_hist=hist is None),
        grid=(S, T // tt),
        in_specs=[hist_spec,
                  pl.BlockSpec((None, tt, C), lambda s, i: (s, i, 0)),
                  pl.BlockSpec((None, width, C), lambda s, i: (l, 0, 0)),
                  vec, vec, vec],
        out_specs=pl.BlockSpec((None, tt, C), lambda s, i: (s, i, 0)),
        out_shape=jax.ShapeDtypeStruct((S, T, C), BF16),
        scratch_shapes=[pltpu.VMEM((HIST_ROWS + tt, C), F32), pltpu.VMEM((tt, C), F32)],
        compiler_params=_cp("arbitrary", "arbitrary"),
    )(hist_arr, u, conv_w, conv_b.reshape(-1, 1, C), ln_g.reshape(-1, 1, C), ln_b.reshape(-1, 1, C))


def _lam(lq1, lk1, lq2, lk2, lam_init):
    a = jnp.exp(jnp.sum(lq1[...] * lk1[...], axis=-1, keepdims=True))
    b = jnp.exp(jnp.sum(lq2[...] * lk2[...], axis=-1, keepdims=True))
    return a - b + lam_init


def _subln(o, g, lam_init):
    on = o * lax.rsqrt(jnp.mean(o * o, axis=-1, keepdims=True) + 1e-6) * g
    return on * (1.0 - lam_init)


def _diff_prompt_kernel(q_ref, k_ref, v_ref, lq1, lk1, lq2, lk2, gs_ref, o_ref,
                        kb_ref, vb_ref, acc_ref, *, tq, lam_init):
    qi = pl.program_id(2)
    scale = HEAD_DIM ** -0.5

    @pl.when(qi == 0)
    def _():
        kb_ref[...] = k_ref[...].astype(BF16)
        vb_ref[...] = v_ref[...].astype(BF16)

    q = q_ref[...]
    qs = (q[:, :HEAD_DIM], q[:, HEAD_DIM:])
    acc_ref[...] = jnp.zeros_like(acc_ref)
    row = qi * tq + _iota((tq, tq), 0)
    col0 = _iota((tq, tq), 1)

    def body(kc, carry):
        k0 = pl.multiple_of(kc * tq, tq)
        kblk = kb_ref[pl.ds(k0, tq), :]
        vblk = vb_ref[pl.ds(k0, tq), :]
        mask = (k0 + col0) <= row
        new = []
        for c in range(2):
            m_old, l_old = carry[2 * c], carry[2 * c + 1]
            s = lax.dot_general(qs[c], kblk[:, c * HEAD_DIM:(c + 1) * HEAD_DIM], NT_DIMS,
                                preferred_element_type=F32) * scale
            s = jnp.where(mask, s, NEG)
            m_new = jnp.maximum(m_old, jnp.max(s, axis=-1, keepdims=True))
            p = jnp.exp(s - m_new)
            alpha = jnp.exp(m_old - m_new)
            l_new = alpha * l_old + jnp.sum(p, axis=-1, keepdims=True)
            acc_ref[c] = alpha * acc_ref[c] + jnp.dot(p.astype(BF16), vblk, preferred_element_type=F32)
            new += [m_new, l_new]
        return tuple(new)

    init = (jnp.full((tq, 1), NEG, F32), jnp.zeros((tq, 1), F32)) * 2
    m1, l1, m2, l2 = lax.fori_loop(0, qi + 1, body, init)
    lam = _lam(lq1, lk1, lq2, lk2, lam_init)
    o = acc_ref[0] / l1 - lam * (acc_ref[1] / l2)
    o_ref[...] = _subln(o, gs_ref[...], lam_init).astype(o_ref.dtype)


def _diff_prompt(dq, dk, dv, lam_params, g_subln, l, S, T, lam_init):
    M, DW = dq.shape
    H = DW // (2 * HEAD_DIM)
    tq = _pick(T, (256, 128, 64))
    nq = T // tq
    hw = 2 * HEAD_DIM
    vec = pl.BlockSpec((None, 1, HEAD_DIM), lambda s, h, i: (l, 0, 0))
    return pl.pallas_call(
        functools.partial(_diff_prompt_kernel, tq=tq, lam_init=lam_init),
        grid=(S, H, nq),
        in_specs=[pl.BlockSpec((tq, hw), lambda s, h, i: (s * nq + i, h)),
                  pl.BlockSpec((T, hw), lambda s, h, i: (s, h)),
                  pl.BlockSpec((T, hw), lambda s, h, i: (s, h)),
                  vec, vec, vec, vec,
                  pl.BlockSpec((None, 1, hw), lambda s, h, i: (l, 0, 0))],
        out_specs=pl.BlockSpec((tq, hw), lambda s, h, i: (s * nq + i, h)),
        out_shape=jax.ShapeDtypeStruct((M, DW), BF16),
        scratch_shapes=[pltpu.VMEM((T, hw), BF16), pltpu.VMEM((T, hw), BF16), pltpu.VMEM((2, tq, hw), F32)],
        compiler_params=_cp("arbitrary", "arbitrary", "arbitrary"),
    )(dq, dk, dv, *[p.reshape(-1, 1, HEAD_DIM) for p in lam_params], g_subln.reshape(-1, 1, hw))


def _diag_blocks(pv, sel_of_row, n, width):
    out = jnp.zeros((pv.shape[0], width), F32)
    for j in range(n):
        out = out + jnp.where(sel_of_row == j, pv[:, j * width:(j + 1) * width], 0.0)
    return out


def _diff_sample_kernel(pt_ref, qbd_ref, kc_ref, vc_ref, kn_ref, vn_ref, lq1, lk1, lq2, lk2, gs_ref, o_ref,
                        m_ref, l_ref, acc_ref, *, n_pages, n_new, n_heads, lam_init):
    p = pl.program_id(1)
    scale = HEAD_DIM ** -0.5
    hw = 2 * HEAD_DIM
    R = qbd_ref.shape[0]
    half = R // 2
    qbd = qbd_ref[...]
    head_of_row = _vmod(_iota((R, hw), 0), n_heads)

    @pl.when(p == 0)
    def _():
        m_ref[...] = jnp.full_like(m_ref, NEG)
        l_ref[...] = jnp.zeros_like(l_ref)
        acc_ref[...] = jnp.zeros_like(acc_ref)

    def step(k_ref, v_ref, valid):
        kb = k_ref[...].astype(BF16)
        vb = v_ref[...].astype(BF16)
        s = lax.dot_general(qbd, kb, NT_DIMS, preferred_element_type=F32) * scale
        if valid is not None:
            s = jnp.where(valid, s, NEG)
        m_old = m_ref[...]
        m_new = jnp.maximum(m_old, jnp.max(s, axis=-1, keepdims=True))
        pr = jnp.exp(s - m_new)
        if valid is not None:
            pr = jnp.where(valid, pr, 0.0)
        alpha = jnp.exp(m_old - m_new)
        l_ref[...] = alpha * l_ref[...] + jnp.sum(pr, axis=-1, keepdims=True)
        pv = jnp.dot(pr.astype(BF16), vb, preferred_element_type=F32)
        acc_ref[...] = alpha * acc_ref[...] + _diag_blocks(pv, head_of_row, n_heads, hw)
        m_ref[...] = m_new

    @pl.when(p < n_pages)
    def _():
        step(kc_ref, vc_ref, None)

    @pl.when(p == n_pages)
    def _():
        page = kn_ref.shape[0]
        j = _iota((R, page), 1)
        t = _vmod(_vdiv(_iota((R, page), 0), n_heads), n_new)
        step(kn_ref, vn_ref, (j < n_new) & (j <= t))
        lam = _lam(lq1, lk1, lq2, lk2, lam_init)
        on = acc_ref[...] / l_ref[...]
        o = on[:half] - lam * on[half:]
        o_ref[...] = _subln(o, gs_ref[...], lam_init)


def _diff_sample(qbd, cache_k, cache_v, k_new, v_new, page_table, lam_params, g_subln, l, n_new, lam_init):
    S, R, DW = qbd.shape
    H = DW // (2 * HEAD_DIM)
    hw = 2 * HEAD_DIM
    n_pages = page_table.shape[1]
    page = cache_k.shape[2]
    vec = pl.BlockSpec((None, 1, HEAD_DIM), lambda s, p, pt: (l, 0, 0))
    cache_spec = pl.BlockSpec((None, None, page, DW),
                              lambda s, p, pt: (l, pt[s, jnp.minimum(p, n_pages - 1)], 0, 0))
    new_spec = pl.BlockSpec((None, page, DW), lambda s, p, pt: (s, 0, 0))
    grid_spec = pltpu.PrefetchScalarGridSpec(
        num_scalar_prefetch=1,
        grid=(S, n_pages + 1),
        in_specs=[pl.BlockSpec((None, R, DW), lambda s, p, pt: (s, 0, 0)),
                  cache_spec, cache_spec, new_spec, new_spec,
                  vec, vec, vec, vec,
                  pl.BlockSpec((None, 1, hw), lambda s, p, pt: (l, 0, 0))],
        out_specs=pl.BlockSpec((None, R // 2, hw), lambda s, p, pt: (s, 0, 0)),
        scratch_shapes=[pltpu.VMEM((R, 1), F32), pltpu.VMEM((R, 1), F32), pltpu.VMEM((R, hw), F32)],
    )
    return pl.pallas_call(
        functools.partial(_diff_sample_kernel, n_pages=n_pages, n_new=n_new, n_heads=H, lam_init=lam_init),
        grid_spec=grid_spec,
        out_shape=jax.ShapeDtypeStruct((S, R // 2, hw), F32),
        compiler_params=_cp("arbitrary", "arbitrary"),
    )(page_table, qbd, cache_k, cache_v, k_new, v_new,
      *[p.reshape(-1, 1, HEAD_DIM) for p in lam_params], g_subln.reshape(-1, 1, hw))


def _compress_kernel(*refs, n_src, has_prefetch):
    if has_prefetch:
        refs = refs[1:]
    k_srcs = refs[:n_src]
    v_srcs = refs[n_src:2 * n_src]
    wk_ref, wv_ref, ok_ref, ov_ref = refs[2 * n_src:]
    for srcs, w_ref, o_ref in ((k_srcs, wk_ref, ok_ref), (v_srcs, wv_ref, ov_ref)):
        w = w_ref[...]
        n = 0
        for src in srcs:
            for b0 in range(0, src.shape[0], NSA_BLOCK):
                o_ref[n:n + 1, :] = jnp.sum(src[b0:b0 + NSA_BLOCK, :] * w, axis=0, keepdims=True)
                n += 1


def _compress_prompt(kv, wc_k, wc_v, l):
    M = kv.shape[0]
    kvw = wc_k.shape[-1]
    rows = 512
    w_spec = pl.BlockSpec((None, NSA_BLOCK, kvw), lambda i: (l, 0, 0))
    o_spec = pl.BlockSpec((rows // NSA_BLOCK, kvw), lambda i: (i, 0))
    o_shape = jax.ShapeDtypeStruct((M // NSA_BLOCK, kvw), F32)
    return pl.pallas_call(
        functools.partial(_compress_kernel, n_src=1, has_prefetch=False),
        grid=(M // rows,),
        in_specs=[pl.BlockSpec((rows, kvw), lambda i: (i, 0)), pl.BlockSpec((rows, kvw), lambda i: (i, 1)),
                  w_spec, w_spec],
        out_specs=[o_spec, o_spec], out_shape=[o_shape, o_shape],
        compiler_params=_cp("arbitrary"),
    )(kv, kv, wc_k, wc_v)


def _compress_sample(cache_k, cache_v, page_table, wc_k, wc_v, l):
    S, n_pages = page_table.shape
    page, kvw = cache_k.shape[2], cache_k.shape[3]
    per_step = (8 * NSA_BLOCK) // page
    assert n_pages % per_step == 0

    def cspec(i):
        return pl.BlockSpec((None, None, page, kvw), lambda s, t, pt: (l, pt[s, t * per_step + i], 0, 0))
    w_spec = pl.BlockSpec((None, NSA_BLOCK, kvw), lambda s, t, pt: (l, 0, 0))
    o_spec = pl.BlockSpec((None, 8, kvw), lambda s, t, pt: (s, t, 0))
    o_shape = jax.ShapeDtypeStruct((S, n_pages * page // NSA_BLOCK, kvw), F32)
    grid_spec = pltpu.PrefetchScalarGridSpec(
        num_scalar_prefetch=1, grid=(S, n_pages // per_step),
        in_specs=[cspec(i) for i in range(per_step)] * 2 + [w_spec, w_spec],
        out_specs=[o_spec, o_spec])
    return pl.pallas_call(
        functools.partial(_compress_kernel, n_src=per_step, has_prefetch=True),
        grid_spec=grid_spec, out_shape=[o_shape, o_shape],
        compiler_params=_cp("arbitrary", "arbitrary"),
    )(page_table, *([cache_k] * per_step), *([cache_v] * per_step), wc_k, wc_v)


def _rank_select(score, n_keep):
    R, N = score.shape
    n_i = _iota((R, N), 1)
    rank = jnp.zeros((R, N), F32)
    for m in range(N):
        col = score[:, m:m + 1]
        beats = jnp.where(col > score, 1.0, jnp.where(col == score, jnp.where(n_i > m, 1.0, 0.0), 0.0))
        rank = rank + beats
    return jnp.where(rank < n_keep, 1.0, 0.0)


def _softmax_rows(s):
    m = jnp.max(s, axis=-1, keepdims=True)
    e = jnp.exp(s - m)
    return e / jnp.sum(e, axis=-1, keepdims=True)


def _nsa_prompt_kernel(q_ref, kcb_ref, vcb_ref, sk_ref, sv_ref, wk_ref, wv_ref, g_ref, o_ref,
                       skb, svb, wkb, wvb, acc_ref, *, tq, hg):
    qi = pl.program_id(2)
    scale = HEAD_DIM ** -0.5
    nblk = kcb_ref.shape[0]
    R = hg * tq
    tk = 2 * tq

    @pl.when(qi == 0)
    def _():
        skb[...] = sk_ref[...].astype(BF16)
        svb[...] = sv_ref[...].astype(BF16)
        wkb[...] = wk_ref[...].astype(BF16)
        wvb[...] = wv_ref[...].astype(BF16)

    q = jnp.concatenate([q_ref[:, hh * HEAD_DIM:(hh + 1) * HEAD_DIM] for hh in range(hg)], axis=0)

    kcb = kcb_ref[...].astype(BF16)
    vcb = vcb_ref[...].astype(BF16)
    t_rn = qi * tq + _vmod(_iota((R, nblk), 0), tq)
    n_rn = _iota((R, nblk), 1)
    complete = (n_rn + 1) * NSA_BLOCK <= t_rn + 1
    s_c = lax.dot_general(q, kcb, NT_DIMS, preferred_element_type=F32) * scale
    p_c = jnp.where(complete, _softmax_rows(jnp.where(complete, s_c, NEG)), 0.0)
    o_c = jnp.dot(p_c.astype(BF16), vcb, preferred_element_type=F32)

    imp = p_c[0:tq]
    for hh in range(1, hg):
        imp = imp + p_c[hh * tq:(hh + 1) * tq]
    t_tn = qi * tq + _iota((tq, nblk), 0)
    n_tn = _iota((tq, nblk), 1)
    complete_t = (n_tn + 1) * NSA_BLOCK <= t_tn + 1
    score = jnp.where(n_tn == _vdiv(t_tn, NSA_BLOCK), FORCE, jnp.where(complete_t, imp, -1.0))
    sel = _rank_select(score, min(NSA_TOPK, nblk)) * jnp.where(score >= 0.0, 1.0, 0.0)
    sel_r = jnp.concatenate([sel] * hg, axis=0).astype(BF16)

    def attend(kb_ref, vb_ref, slot, width, lo, hi, valid_fn):
        acc_ref[slot] = jnp.zeros((R, HEAD_DIM), F32)
        t_rk = qi * tq + _vmod(_iota((R, width), 0), tq)
        j_rk = _iota((R, width), 1)

        def body(kc, carry):
            m_old, l_old = carry
            k0 = pl.multiple_of(kc * width, width)
            kb = kb_ref[pl.ds(k0, width), :]
            vb = vb_ref[pl.ds(k0, width), :]
            s = lax.dot_general(q, kb, NT_DIMS, preferred_element_type=F32) * scale
            valid = valid_fn(k0, k0 + j_rk, t_rk)
            s = jnp.where(valid, s, NEG)
            m_new = jnp.maximum(m_old, jnp.max(s, axis=-1, keepdims=True))
            p = jnp.where(valid, jnp.exp(s - m_new), 0.0)
            alpha = jnp.exp(m_old - m_new)
            acc_ref[slot] = alpha * acc_ref[slot] + jnp.dot(p.astype(BF16), vb, preferred_element_type=F32)
            return m_new, alpha * l_old + jnp.sum(p, axis=-1, keepdims=True)

        _, l_fin = lax.fori_loop(lo, hi, body, (jnp.full((R, 1), NEG, F32), jnp.zeros((R, 1), F32)))
        return acc_ref[slot] / l_fin

    def sel_valid(k0, kpos, t_rk):
        blk = _vdiv(k0 + _iota((nblk, tk), 1), NSA_BLOCK)
        expand = jnp.where(_iota((nblk, tk), 0) == blk, 1.0, 0.0).astype(BF16)
        selk = jnp.dot(sel_r, expand, preferred_element_type=F32)
        return (selk > 0.5) & (kpos <= t_rk)

    def win_valid(k0, kpos, t_rk):
        return (kpos <= t_rk) & (kpos > t_rk - WINDOW)

    q_end = qi * tq + tq
    o_s = attend(skb, svb, 0, tk, 0, (q_end + tk - 1) // tk, sel_valid)
    o_w = attend(wkb, wvb, 1, tq, jnp.maximum(qi * tq - (WINDOW - 1), 0) // tq, qi + 1, win_valid)

    g = g_ref[...]
    for hh in range(hg):
        rs = slice(hh * tq, (hh + 1) * tq)
        o = (g[:, 3 * hh:3 * hh + 1] * o_c[rs] + g[:, 3 * hh + 1:3 * hh + 2] * o_s[rs]
             + g[:, 3 * hh + 2:3 * hh + 3] * o_w[rs])
        o_ref[:, hh * HEAD_DIM:(hh + 1) * HEAD_DIM] = o.astype(o_ref.dtype)


def _nsa_prompt(nq, kcb, vcb, kv, gates, S, T, hg):
    M, QW = nq.shape
    tq = _pick(T, (128, 64))
    nqt = T // tq
    nblk = T // NSA_BLOCK
    gw = hg * HEAD_DIM

    def kv_spec(seg):
        return pl.BlockSpec((T, HEAD_DIM), lambda s, g, i: (s, seg * G_NSA + g))
    cb_spec = pl.BlockSpec((nblk, HEAD_DIM), lambda s, g, i: (s, g))
    return pl.pallas_call(
        functools.partial(_nsa_prompt_kernel, tq=tq, hg=hg),
        grid=(S, G_NSA, nqt),
        in_specs=[pl.BlockSpec((tq, gw), lambda s, g, i: (s * nqt + i, g)),
                  cb_spec, cb_spec, kv_spec(2), kv_spec(3), kv_spec(4), kv_spec(5),
                  pl.BlockSpec((tq, LANE), lambda s, g, i: (s * nqt + i, g))],
        out_specs=pl.BlockSpec((tq, gw), lambda s, g, i: (s * nqt + i, g)),
        out_shape=jax.ShapeDtypeStruct((M, QW), BF16),
        scratch_shapes=[pltpu.VMEM((T, HEAD_DIM), BF16) for _ in range(4)]
        + [pltpu.VMEM((2, hg * tq, HEAD_DIM), F32)],
        compiler_params=_cp("arbitrary", "arbitrary", "arbitrary"),
    )(nq, kcb, vcb, kv, kv, kv, kv, gates)


def _nsa_sample_kernel(pt_ref, qbd_ref, kcb_ref, vcb_ref, sk_ref, sv_ref, skn_ref, svn_ref, wkn_ref, wvn_ref,
                       wk_ref, wv_ref, g_ref, o_ref, sel_ref, oc_ref, m_ref, l_ref, acc_ref, *,
                       n_pages, n_new, hg):
    p = pl.program_id(1)
    scale = HEAD_DIM ** -0.5
    R = qbd_ref.shape[0]
    page = sk_ref.shape[0]
    nblk_past = kcb_ref.shape[0]
    bpp = page // NSA_BLOCK
    qbd = qbd_ref[...]
    grp_of_row = _vmod(_iota((R, HEAD_DIM), 0), G_NSA)

    def extract(pv):
        return _diag_blocks(pv, grp_of_row, G_NSA, HEAD_DIM)

    @pl.when(p == 0)
    def _():
        kcb = kcb_ref[...].astype(BF16)
        vcb = vcb_ref[...].astype(BF16)
        p_c = _softmax_rows(lax.dot_general(qbd, kcb, NT_DIMS, preferred_element_type=F32) * scale)
        oc_ref[...] = extract(jnp.dot(p_c.astype(BF16), vcb, preferred_element_type=F32))
        rows = R // hg
        imp = p_c[0:rows]
        for hh in range(1, hg):
            imp = imp + p_c[hh * rows:(hh + 1) * rows]
        sel = _rank_select(imp, min(NSA_TOPK, nblk_past + 1) - 1)
        sel_ref[...] = jnp.concatenate([sel] * hg, axis=0).astype(BF16)
        m_ref[...] = jnp.full_like(m_ref, NEG)
        l_ref[...] = jnp.zeros_like(l_ref)
        acc_ref[...] = jnp.zeros_like(acc_ref)

    def step(k_ref, v_ref, valid):
        s = lax.dot_general(qbd, k_ref[...].astype(BF16), NT_DIMS, preferred_element_type=F32) * scale
        s = jnp.where(valid, s, NEG)
        m_old = m_ref[...]
        m_new = jnp.maximum(m_old, jnp.max(s, axis=-1, keepdims=True))
        pr = jnp.where(valid, jnp.exp(s - m_new), 0.0)
        alpha = jnp.exp(m_old - m_new)
        l_ref[...] = alpha * l_ref[...] + jnp.sum(pr, axis=-1, keepdims=True)
        pv = jnp.dot(pr.astype(BF16), v_ref[...].astype(BF16), preferred_element_type=F32)
        acc_ref[...] = alpha * acc_ref[...] + extract(pv)
        m_ref[...] = m_new

    blk = p * bpp + _vdiv(_iota((nblk_past, page), 1), NSA_BLOCK)
    expand = jnp.where(_iota((nblk_past, page), 0) == blk, 1.0, 0.0).astype(BF16)
    selk = jnp.dot(sel_ref[...], expand, preferred_element_type=F32)
    step(sk_ref, sv_ref, selk > 0.5)

    @pl.when(p == n_pages - 1)
    def _():
        j = _iota((R, page), 1)
        t = _vmod(_vdiv(_iota((R, page), 0), G_NSA), n_new)
        new_valid = (j < n_new) & (j <= t)
        step(skn_ref, svn_ref, new_valid)
        o_s = acc_ref[...] / l_ref[...]
        wb = wk_ref.shape[0]
        jw = _iota((R, wb), 1)
        tw = _vmod(_vdiv(_iota((R, wb), 0), G_NSA), n_new)
        buf_valid = jw > tw + (wb - WINDOW)
        s_b = lax.dot_general(qbd, wk_ref[...].astype(BF16), NT_DIMS, preferred_element_type=F32) * scale
        s_n = lax.dot_general(qbd, wkn_ref[...].astype(BF16), NT_DIMS, preferred_element_type=F32) * scale
        s_b = jnp.where(buf_valid, s_b, NEG)
        s_n = jnp.where(new_valid, s_n, NEG)
        m = jnp.maximum(jnp.max(s_b, axis=-1, keepdims=True), jnp.max(s_n, axis=-1, keepdims=True))
        p_b = jnp.where(buf_valid, jnp.exp(s_b - m), 0.0)
        p_n = jnp.where(new_valid, jnp.exp(s_n - m), 0.0)
        lw = jnp.sum(p_b, axis=-1, keepdims=True) + jnp.sum(p_n, axis=-1, keepdims=True)
        pv = (jnp.dot(p_b.astype(BF16), wv_ref[...].astype(BF16), preferred_element_type=F32)
              + jnp.dot(p_n.astype(BF16), wvn_ref[...].astype(BF16), preferred_element_type=F32))
        o_w = extract(pv) / lw
        g = g_ref[...]
        o_ref[...] = g[:, 0:1] * oc_ref[...] + g[:, 1:2] * o_s + g[:, 2:3] * o_w


def _nsa_sample(qbd, kcb, vcb, cache_sk, cache_sv, sk_new, sv_new, wk_new, wv_new, win_k, win_v, gates_r,
                page_table, l, n_new, hg):
    S, R, kvw = qbd.shape
    n_pages = page_table.shape[1]
    page = cache_sk.shape[2]
    nblk_past = kcb.shape[1]
    wb = win_k.shape[2]
    per_s = lambda *shape: pl.BlockSpec((None,) + shape, lambda s, p, pt: (s, 0, 0))
    cache_spec = pl.BlockSpec((None, None, page, kvw), lambda s, p, pt: (l, pt[s, p], 0, 0))
    win_spec = pl.BlockSpec((None, None, wb, kvw), lambda s, p, pt: (l, s, 0, 0))
    grid_spec = pltpu.PrefetchScalarGridSpec(
        num_scalar_prefetch=1, grid=(S, n_pages),
        in_specs=[per_s(R, kvw), per_s(nblk_past, kvw), per_s(nblk_past, kvw), cache_spec, cache_spec,
                  per_s(page, kvw), per_s(page, kvw), per_s(page, kvw), per_s(page, kvw),
                  win_spec, win_spec, per_s(R, LANE)],
        out_specs=per_s(R, HEAD_DIM),
        scratch_shapes=[pltpu.VMEM((R, nblk_past), BF16), pltpu.VMEM((R, HEAD_DIM), F32),
                        pltpu.VMEM((R, 1), F32), pltpu.VMEM((R, 1), F32), pltpu.VMEM((R, HEAD_DIM), F32)])
    return pl.pallas_call(
        functools.partial(_nsa_sample_kernel, n_pages=n_pages, n_new=n_new, hg=hg),
        grid_spec=grid_spec,
        out_shape=jax.ShapeDtypeStruct((S, R, HEAD_DIM), F32),
        compiler_params=_cp("arbitrary", "arbitrary"),
    )(page_table, qbd, kcb, vcb, cache_sk, cache_sv, sk_new, sv_new, wk_new, wv_new, win_k, win_v, gates_r)


def kernel(x_prompt, x_sample, cache_diff_k, cache_diff_v, cache_cmp_k, cache_cmp_v, cache_sel_k, cache_sel_v, state_win_k, state_win_v, state_conv, page_table, c_prompt, c_sample, w_ada, b_ada, g_pre_mix, g_post_mix, g_pre_ffn, g_post_ffn, w_in, conv_w, conv_b, conv_ln_g, conv_ln_b, w_conv_out, lam_q1, lam_k1, lam_q2, lam_k2, g_subln, w_diff_out, w_cmp_k, w_cmp_v, w_nsa_out, w_out, w_gate_up, w_down):
    B, T, D = x_prompt.shape
    Bs, Ts, _ = x_sample.shape
    depth = w_in.shape[0]
    C = conv_w.shape[2]
    DW = w_diff_out.shape[1]
    H = DW // (2 * HEAD_DIM)
    QW = w_nsa_out.shape[1]
    hg = QW // (HEAD_DIM * G_NSA)
    KVW = G_NSA * HEAD_DIM
    n_gate = 3 * G_NSA * hg
    d_ff = w_down.shape[1]
    n_pool, page = cache_diff_k.shape[1], cache_diff_k.shape[2]
    wb = state_win_k.shape[2]
    cw = conv_w.shape[1]
    off_dq = 2 * C
    off_dk = off_dq + DW
    off_dv = off_dk + DW
    off_nq = off_dv + DW
    off_kv = off_nq + QW
    off_ng = off_kv + 6 * KVW
    off_mg = off_ng + n_gate
    assert off_mg + 3 * D == w_in.shape[2]

    w_mg = w_in[:, :, off_mg:]
    w_ng = w_in[:, :, off_ng:off_mg].reshape(depth, D, G_NSA, 3 * hg)
    w_ng = jnp.pad(w_ng, ((0, 0), (0, 0), (0, 0), (0, LANE - 3 * hg))).reshape(depth, D, G_NSA * LANE)
    w_down_b = _cast_bf16(w_down)
    wc_k = w_cmp_k.reshape(depth, NSA_BLOCK, KVW)
    wc_v = w_cmp_v.reshape(depth, NSA_BLOCK, KVW)
    cdk = cache_diff_k.reshape(depth, n_pool, page, DW)
    cdv = cache_diff_v.reshape(depth, n_pool, page, DW)
    cck = cache_cmp_k.reshape(depth, n_pool, page, KVW)
    ccv = cache_cmp_v.reshape(depth, n_pool, page, KVW)
    csk = cache_sel_k.reshape(depth, n_pool, page, KVW)
    csv = cache_sel_v.reshape(depth, n_pool, page, KVW)
    swk = state_win_k.reshape(depth, Bs, wb, KVW)
    swv = state_win_v.reshape(depth, Bs, wb, KVW)

    n_seq = B + Bs
    c_all = jnp.pad(jnp.concatenate([c_prompt, c_sample], axis=0), ((0, -n_seq % 8), (0, 0)))
    mod = _ada(c_all, w_ada, b_ada)
    mod_p = mod[:, :B].reshape(depth, B, 1, 6 * D)
    mod_s = jnp.repeat(mod[:, B:n_seq], Ts, axis=1).reshape(depth, 1, Bs * Ts, 6 * D)

    Mp, Ms = B * T, Bs * Ts
    xp = x_prompt
    xs = x_sample.reshape(1, Ms, D)
    (hp,) = _norm(xp, mod_p, g_pre=g_pre_mix, l_pre=0, sc_chunk=1, sh_chunk=0)
    (hs,) = _norm(xs, mod_s, g_pre=g_pre_mix, l_pre=0, sc_chunk=1, sh_chunk=0)

    eye_h = jnp.eye(H, dtype=BF16)
    eye_2 = jnp.eye(2, dtype=BF16)
    eye_g = jnp.eye(G_NSA, dtype=BF16)
    outs = {n: [] for n in ("p_dk", "p_dv", "p_ck", "p_cv", "p_sk", "p_sv", "p_wk", "p_wv", "p_conv",
                            "s_dk", "s_dv", "s_ck", "s_cv", "s_sk", "s_sv", "s_wk", "s_wv", "s_conv")}

    for l in range(depth):
        lam_init = 0.8 - 0.6 * math.exp(-0.3 * l)
        lam_params = (lam_q1, lam_k1, lam_q2, lam_k2)

        def project(h2d):
            u = _mm(h2d, w_in, l, 0, C, out_dtype=F32, epilogue="glu", pair_off=C, tn=256)
            dq = _mm(h2d, w_in, l, off_dq, DW, out_dtype=BF16)
            dk = _mm(h2d, w_in, l, off_dk, DW, out_dtype=F32)
            dv = _mm(h2d, w_in, l, off_dv, DW, out_dtype=F32)
            nq = _mm(h2d, w_in, l, off_nq, QW, out_dtype=BF16)
            kv = _mm(h2d, w_in, l, off_kv, 6 * KVW, out_dtype=F32)
            sg = _mm(h2d, w_ng, l, 0, G_NSA * LANE, out_dtype=F32, epilogue="sigmoid")
            smg = _mm(h2d, w_mg, l, 0, 3 * D, out_dtype=BF16, epilogue="sigmoid")
            return u, dq, dk, dv, nq, kv, sg, smg

        def finish(x3, mod_g, a_conv, a_diff, a_nsa, smg):
            merged = _merge_mm(a_conv, a_diff, a_nsa, w_conv_out, w_diff_out, w_nsa_out, smg, l)
            y = _mm(merged, w_out, l, 0, D, out_dtype=F32)
            x1, h2 = _norm(x3, mod_g, y=y.reshape(x3.shape), g_post=g_post_mix, l_post=l, gate_chunk=2,
                           g_pre=g_pre_ffn, l_pre=l, sc_chunk=4, sh_chunk=3)
            act = _mm(h2.reshape(-1, D), w_gate_up, l, 0, d_ff, out_dtype=BF16, epilogue="swiglu",
                      pair_off=d_ff, tn=256)
            f = _mm(act, w_down_b, l, 0, D, out_dtype=F32)
            if l + 1 < depth:
                return _norm(x1, mod_g, y=f.reshape(x3.shape), g_post=g_post_ffn, l_post=l, gate_chunk=5,
                             g_pre=g_pre_mix, l_pre=l + 1, sc_chunk=1, sh_chunk=0)
            (x2,) = _norm(x1, mod_g, y=f.reshape(x3.shape), g_post=g_post_ffn, l_post=l, gate_chunk=5)
            return x2, None

        u, dq, dk, dv, nq, kv, sg, smg = project(hp.reshape(Mp, D))
        a_conv = _conv_module(u.reshape(B, T, C), None, conv_w, conv_b, conv_ln_g, conv_ln_b, l)
        a_diff = _diff_prompt(dq, dk, dv, lam_params, g_subln, l, B, T, lam_init)
        kcb, vcb = _compress_prompt(kv, wc_k, wc_v, l)
        a_nsa = _nsa_prompt(nq, kcb, vcb, kv, sg, B, T, hg)
        xp, hp = finish(xp, mod_p, a_conv.reshape(Mp, C), a_diff, a_nsa, smg)
        kv5 = kv.reshape(B, T, 6, G_NSA, HEAD_DIM)
        keep = min(WINDOW, T)
        outs["p_dk"].append(dk.reshape(B, T, H, 2 * HEAD_DIM))
        outs["p_dv"].append(dv.reshape(B, T, H, 2 * HEAD_DIM))
        for i, n in enumerate(("p_ck", "p_cv", "p_sk", "p_sv")):
            outs[n].append(kv5[:, :, i])
        outs["p_wk"].append(kv5[:, T - keep:, 4])
        outs["p_wv"].append(kv5[:, T - keep:, 5])
        outs["p_conv"].append(u.reshape(B, T, C)[:, T - (cw - 1):])

        u, dq, dk, dv, nq, kv, sg, smg = project(hs.reshape(Ms, D))
        u3 = u.reshape(Bs, Ts, C)
        hist = jnp.pad(state_conv[l], ((0, 0), (HIST_ROWS - (cw - 1), 0), (0, 0)))
        a_conv = _conv_module(jnp.pad(u3, ((0, 0), (0, 16 - Ts), (0, 0))), hist,
                              conv_w, conv_b, conv_ln_g, conv_ln_b, l)[:, :Ts].reshape(Ms, C)

        def pad_page(a):
            return jnp.pad(a.reshape(Bs, Ts, -1), ((0, 0), (0, page - Ts), (0, 0)))

        q5 = dq.reshape(Bs, Ts, H, 2, HEAD_DIM)
        qbd = jnp.einsum("bthcd,hi,cj->bcthijd", q5, eye_h, eye_2).reshape(Bs, 2 * Ts * H, DW)
        o_d = _diff_sample(qbd, cdk, cdv, pad_page(dk), pad_page(dv), page_table, lam_params, g_subln, l,
                           Ts, lam_init)
        a_diff = o_d.reshape(Ms, DW).astype(BF16)

        kv5 = kv.reshape(Bs, Ts, 6, G_NSA, HEAD_DIM)
        new6 = [pad_page(kv[:, i * KVW:(i + 1) * KVW]) for i in range(6)]
        kcb, vcb = _compress_sample(cck, ccv, page_table, wc_k, wc_v, l)
        nq5 = nq.reshape(Bs, Ts, G_NSA, hg, HEAD_DIM)
        qbd_n = jnp.einsum("btghd,gi->bhtgid", nq5, eye_g).reshape(Bs, hg * Ts * G_NSA, KVW)
        gr = sg.reshape(Bs, Ts, G_NSA, LANE)[..., :3 * hg].reshape(Bs, Ts, G_NSA, hg, 3)
        gr = jnp.pad(gr.transpose(0, 3, 1, 2, 4).reshape(Bs, hg * Ts * G_NSA, 3), ((0, 0), (0, 0), (0, LANE - 3)))
        o_n = _nsa_sample(qbd_n, kcb, vcb, csk, csv, new6[2], new6[3], new6[4], new6[5], swk, swv, gr,
                          page_table, l, Ts, hg)
        a_nsa = (o_n.reshape(Bs, hg, Ts, G_NSA, HEAD_DIM).transpose(0, 2, 3, 1, 4)
                 .reshape(Ms, QW).astype(BF16))
        xs, hs = finish(xs, mod_s, a_conv, a_diff, a_nsa, smg)
        outs["s_dk"].append(dk.reshape(Bs, Ts, H, 2 * HEAD_DIM))
        outs["s_dv"].append(dv.reshape(Bs, Ts, H, 2 * HEAD_DIM))
        for i, n in enumerate(("s_ck", "s_cv", "s_sk", "s_sv")):
            outs[n].append(kv5[:, :, i])
        outs["s_wk"].append(jnp.concatenate([state_win_k[l], kv5[:, :, 4]], axis=1)[:, Ts:])
        outs["s_wv"].append(jnp.concatenate([state_win_v[l], kv5[:, :, 5]], axis=1)[:, Ts:])
        outs["s_conv"].append(jnp.concatenate([state_conv[l], u3], axis=1)[:, Ts:])

    st = {n: jnp.stack(v) for n, v in outs.items()}
    return (xp, xs.reshape(Bs, Ts, D),
            st["p_dk"], st["p_dv"], st["p_ck"], st["p_cv"], st["p_sk"], st["p_sv"], st["p_wk"], st["p_wv"],
            st["p_conv"],
            st["s_dk"], st["s_dv"], st["s_ck"], st["s_cv"], st["s_sk"], st["s_sv"], st["s_wk"], st["s_wv"],
            st["s_conv"])
```

```python
import functools
import math

import jax
import jax.numpy as jnp
from jax import lax
from jax.experimental import pallas as pl
from jax.experimental.pallas import tpu as pltpu

F32 = jnp.float32
BF16 = jnp.bfloat16

HEAD_DIM = 128
G_NSA = 4
NSA_BLOCK = 64
NSA_TOPK = 16
WINDOW = 512
NEG = -1e30
FORCE = 1e9
LANE = 128
SUBLANE = 8
V7X_VMEM_LIMIT = 60 * 1024 * 1024
NT_DIMS = (((1,), (1,)), ((), ()))
QK_SCALE = HEAD_DIM ** -0.5


def _cp(*sem):
    return pltpu.CompilerParams(dimension_semantics=sem, vmem_limit_bytes=V7X_VMEM_LIMIT)


def _pick(n, cands):
    for c in cands:
        if n % c == 0:
            return c
    return n


def _sigmoid(x):
    return 1.0 / (1.0 + jnp.exp(-x))


def _vdiv(x, n):
    assert n & (n - 1) == 0
    return lax.shift_right_logical(x, jnp.int32(n.bit_length() - 1))


def _vmod(x, n):
    assert n & (n - 1) == 0
    return lax.bitwise_and(x, jnp.int32(n - 1))


def _iota(shape, dim):
    return lax.broadcasted_iota(jnp.int32, shape, dim)


def _cast_weights(w_refs, wn_refs, wb_refs, cast_rows, shift):
    k_dim, tn = wb_refs[0].shape

    def body(c, carry):
        r0 = pl.multiple_of(c * cast_rows, cast_rows)
        for i, (w, wb) in enumerate(zip(w_refs, wb_refs)):
            wv = w[pl.ds(r0, cast_rows), :]
            if shift:
                wv = jnp.concatenate([wv, wn_refs[i][pl.ds(r0, cast_rows), :]], axis=1)[:, shift:shift + tn]
            wb[pl.ds(r0, cast_rows), :] = wv.astype(BF16)
        return carry
    lax.fori_loop(0, k_dim // cast_rows, body, 0)


def _mm_kernel(*refs, n_w, n_out, epilogue, cast_w, cast_rows, shift, out_scale):
    x_ref = refs[0]
    pos = 1
    w_refs = refs[pos:pos + n_w]
    pos += n_w
    wn_refs = refs[pos:pos + n_w] if shift else ()
    pos += n_w if shift else 0
    o_refs = refs[pos:pos + n_out]
    wb_refs = refs[pos + n_out:]
    if cast_w:
        @pl.when(pl.program_id(1) == 0)
        def _():
            _cast_weights(w_refs, wn_refs, wb_refs, cast_rows, shift)
        ws = wb_refs
    else:
        ws = w_refs
    x = x_ref[...]
    accs = [jnp.dot(x, w[...], preferred_element_type=F32) for w in ws]
    if epilogue == "none":
        r = accs[0]
    elif epilogue == "sigmoid":
        r = _sigmoid(accs[0])
    elif epilogue == "glu":
        r = accs[0] * _sigmoid(accs[1])
    else:
        r = accs[0] * _sigmoid(accs[0]) * accs[1]
    if out_scale != 1.0:
        r = r * out_scale
    for o_ref in o_refs:
        o_ref[...] = r.astype(o_ref.dtype)


def _mm(x, w, l, col_off, n_cols, *, out_dtype, epilogue="none", pair_off=None, tn=512, tm=None, out_scale=1.0,
        bf16_copy=False):
    M, K = x.shape
    n_w = 1 if pair_off is None else 2
    cast_w = w.dtype != BF16
    shift = col_off % LANE
    assert cast_w or not shift
    if tm is None:
        tm = _pick(M, (512,)) if K > 4096 else _pick(M, (1024, 512, 256))
    tm = min(tm, M)
    base = col_off - shift
    assert n_cols % tn == 0 and base % tn == 0 and (pair_off or 0) % tn == 0
    offs = [base // tn] if n_w == 1 else [base // tn, (base + pair_off) // tn]
    in_specs = [pl.BlockSpec((tm, K), lambda j, i: (i, 0))]
    for o in offs:
        in_specs.append(pl.BlockSpec((None, K, tn), lambda j, i, o=o: (l, 0, o + j)))
    if shift:
        for o in offs:
            in_specs.append(pl.BlockSpec((None, K, LANE), lambda j, i, o=o: (l, 0, (o + j + 1) * (tn // LANE))))
    scratch = [pltpu.VMEM((K, tn), BF16) for _ in range(n_w)] if cast_w else []
    dtypes = [out_dtype] + ([BF16] if bf16_copy else [])
    outs = pl.pallas_call(
        functools.partial(_mm_kernel, n_w=n_w, n_out=len(dtypes), epilogue=epilogue, cast_w=cast_w,
                          cast_rows=_pick(K, (256, 128, 64, 16)), shift=shift, out_scale=out_scale),
        grid=(n_cols // tn, M // tm),
        in_specs=in_specs,
        out_specs=[pl.BlockSpec((tm, tn), lambda j, i: (i, j)) for _ in dtypes],
        out_shape=[jax.ShapeDtypeStruct((M, n_cols), d) for d in dtypes],
        scratch_shapes=scratch,
        compiler_params=_cp("arbitrary", "arbitrary"),
    )(x, *([w] * (n_w * (2 if shift else 1))))
    return outs if bf16_copy else outs[0]


def _cast_kernel(x_ref, o_ref):
    o_ref[...] = x_ref[...].astype(o_ref.dtype)


def _cast_bf16(w):
    dep, K, N = w.shape
    tk = _pick(K, (512, 256, 128, 64, 16))
    return pl.pallas_call(
        _cast_kernel,
        grid=(dep, K // tk),
        in_specs=[pl.BlockSpec((None, tk, N), lambda d, k: (d, k, 0))],
        out_specs=pl.BlockSpec((None, tk, N), lambda d, k: (d, k, 0)),
        out_shape=jax.ShapeDtypeStruct(w.shape, BF16),
        compiler_params=_cp("arbitrary", "arbitrary"),
    )(w)


def _merge_kernel(a1, a2, a3, w1, w2, w3, g1, g2, g3, o_ref, wb1, wb2, wb3, *, cast_rows):
    @pl.when(pl.program_id(1) == 0)
    def _():
        _cast_weights((w1, w2, w3), (), (wb1, wb2, wb3), cast_rows, 0)

    acc = g1[...].astype(F32) * jnp.dot(a1[...], wb1[...], preferred_element_type=F32)
    acc = acc + g2[...].astype(F32) * jnp.dot(a2[...], wb2[...], preferred_element_type=F32)
    acc = acc + g3[...].astype(F32) * jnp.dot(a3[...], wb3[...], preferred_element_type=F32)
    o_ref[...] = acc.astype(o_ref.dtype)


def _merge_mm(a_conv, a_diff, a_nsa, w_conv_out, w_diff_out, w_nsa_out, gates, l):
    M, K = a_conv.shape
    D = w_conv_out.shape[2]
    tm = _pick(M, (512, 256))
    tn = 512
    nb = D // tn
    a_spec = pl.BlockSpec((tm, K), lambda j, i: (i, 0))
    w_spec = pl.BlockSpec((None, K, tn), lambda j, i: (l, 0, j))
    g_specs = [pl.BlockSpec((tm, tn), lambda j, i, s=s: (i, s * nb + j)) for s in range(3)]
    return pl.pallas_call(
        functools.partial(_merge_kernel, cast_rows=_pick(K, (256, 128))),
        grid=(nb, M // tm),
        in_specs=[a_spec] * 3 + [w_spec] * 3 + g_specs,
        out_specs=pl.BlockSpec((tm, tn), lambda j, i: (i, j)),
        out_shape=jax.ShapeDtypeStruct((M, D), BF16),
        scratch_shapes=[pltpu.VMEM((K, tn), BF16) for _ in range(3)],
        compiler_params=_cp("arbitrary", "arbitrary"),
    )(a_conv, a_diff, a_nsa, w_conv_out, w_diff_out, w_nsa_out, gates, gates, gates)


def _ada_kernel(c_ref, w_ref, b_ref, o_ref):
    c = c_ref[...]
    a = (c * _sigmoid(c)).astype(BF16)
    o_ref[...] = jnp.dot(a, w_ref[...].astype(BF16), preferred_element_type=F32) + b_ref[...]


def _ada(c_all, w_ada, b_ada):
    dep, D, N = w_ada.shape
    R = c_all.shape[0]
    tn = 512
    return pl.pallas_call(
        _ada_kernel,
        grid=(dep, N // tn),
        in_specs=[pl.BlockSpec((R, D), lambda d, j: (0, 0)),
                  pl.BlockSpec((None, D, tn), lambda d, j: (d, 0, j)),
                  pl.BlockSpec((None, 1, tn), lambda d, j: (d, 0, j))],
        out_specs=pl.BlockSpec((None, R, tn), lambda d, j: (d, 0, j)),
        out_shape=jax.ShapeDtypeStruct((dep, R, N), F32),
        compiler_params=_cp("arbitrary", "arbitrary"),
    )(c_all, w_ada, b_ada.reshape(dep, 1, N))


def _norm_kernel(*refs, has_resid, has_h):
    it = iter(refs)
    x = next(it)[...]
    if has_resid:
        y = next(it)[...]
        g_post = next(it)[...]
        gate = next(it)[...]
    if has_h:
        g_pre = next(it)[...]
        sc = next(it)[...]
        sh = next(it)[...]
    if has_resid:
        yn = y * lax.rsqrt(jnp.mean(y * y, axis=-1, keepdims=True) + 1e-6) * g_post
        x = x + gate * yn
        next(it)[...] = x
    if has_h:
        hn = x * lax.rsqrt(jnp.mean(x * x, axis=-1, keepdims=True) + 1e-6) * g_pre
        next(it)[...] = (hn * (1.0 + sc) + sh).astype(BF16)


def _norm(x, mod, *, y=None, g_post=None, l_post=None, gate_chunk=None,
          g_pre=None, l_pre=None, sc_chunk=None, sh_chunk=None):
    S, T, D = x.shape
    R = mod.shape[2]
    tr = _pick(T, (256, 128, 64, 32, 16, 8))
    mr = 1 if R == 1 else tr
    has_resid = y is not None
    has_h = g_pre is not None
    row = pl.BlockSpec((None, tr, D), lambda s, i: (s, i, 0))

    def gspec(l):
        return pl.BlockSpec((None, 1, D), lambda s, i: (l, 0, 0))

    def mspec(l, k):
        if R == 1:
            return pl.BlockSpec((None, None, 1, D), lambda s, i: (l, s, 0, k))
        return pl.BlockSpec((None, None, mr, D), lambda s, i: (l, s, i, k))

    args, specs = [x], [row]
    if has_resid:
        args += [y, g_post.reshape(-1, 1, D), mod]
        specs += [row, gspec(l_post), mspec(l_post, gate_chunk)]
    if has_h:
        args += [g_pre.reshape(-1, 1, D), mod, mod]
        specs += [gspec(l_pre), mspec(l_pre, sc_chunk), mspec(l_pre, sh_chunk)]
    out_shape, out_specs = [], []
    if has_resid:
        out_shape.append(jax.ShapeDtypeStruct((S, T, D), F32))
        out_specs.append(row)
    if has_h:
        out_shape.append(jax.ShapeDtypeStruct((S, T, D), BF16))
        out_specs.append(row)
    outs = pl.pallas_call(
        functools.partial(_norm_kernel, has_resid=has_resid, has_h=has_h),
        grid=(S, T // tr),
        in_specs=specs, out_specs=out_specs, out_shape=out_shape,
        compiler_params=_cp("arbitrary", "arbitrary"),
    )(*args)
    return outs


HIST_ROWS = 32


def _conv_kernel(hist_ref, cur_ref, w_ref, b_ref, g_ref, be_ref, o_ref, xin_ref, y_ref, *,
                 tt, width, zero_first_hist):
    C = cur_ref.shape[-1]
    hist = hist_ref[...]
    if zero_first_hist:
        hist = jnp.where(pl.program_id(1) > 0, hist, 0.0)
    xin_ref[0:HIST_ROWS, :] = hist
    xin_ref[HIST_ROWS:HIST_ROWS + tt, :] = cur_ref[...]
    cc = _pick(C, (512, 256, 128))
    rc = min(64, tt)
    off = HIST_ROWS - (width - 1)

    def cbody(c, carry):
        c0 = pl.multiple_of(c * cc, cc)
        for r0 in range(0, tt, rc):
            acc = jnp.zeros((rc, cc), F32)
            for j in range(width):
                acc = acc + xin_ref[pl.ds(off + r0 + j, rc), pl.ds(c0, cc)] * w_ref[pl.ds(j, 1), pl.ds(c0, cc)]
            y_ref[pl.ds(r0, rc), pl.ds(c0, cc)] = acc
        return carry
    lax.fori_loop(0, C // cc, cbody, 0)

    rr = min(32, tt)

    def rbody(r, carry):
        r0 = pl.multiple_of(r * rr, rr)
        y = y_ref[pl.ds(r0, rr), :] + b_ref[...]
        mu = jnp.mean(y, axis=-1, keepdims=True)
        yc = y - mu
        var = jnp.mean(yc * yc, axis=-1, keepdims=True)
        yn = yc * lax.rsqrt(var + 1e-5) * g_ref[...] + be_ref[...]
        o_ref[pl.ds(r0, rr), :] = (yn * _sigmoid(yn)).astype(o_ref.dtype)
        return carry
    lax.fori_loop(0, tt // rr, rbody, 0)


def _conv_module(u, hist, conv_w, conv_b, ln_g, ln_b, l):
    S, T, C = u.shape
    width = conv_w.shape[1]
    tt = _pick(T, (256, 128, 64, 32, 16))
    if hist is None:
        hist_arr = u
        hist_spec = pl.BlockSpec((None, HIST_ROWS, C),
                                 lambda s, i: (s, jnp.maximum(i * (tt // HIST_ROWS) - 1, 0), 0))
    else:
        hist_arr = hist
        hist_spec = pl.BlockSpec((None, HIST_ROWS, C), lambda s, i: (s, 0, 0))
    vec = pl.BlockSpec((None, 1, C), lambda s, i: (l, 0, 0))
    return pl.pallas_call(
        functools.partial(_conv_kernel, tt=tt, width=width, zero_first_hist=hist is None),
        grid=(S, T // tt),
        in_specs=[hist_spec,
                  pl.BlockSpec((None, tt, C), lambda s, i: (s, i, 0)),
                  pl.BlockSpec((None, width, C), lambda s, i: (l, 0, 0)),
                  vec, vec, vec],
        out_specs=pl.BlockSpec((None, tt, C), lambda s, i: (s, i, 0)),
        out_shape=jax.ShapeDtypeStruct((S, T, C), BF16),
        scratch_shapes=[pltpu.VMEM((HIST_ROWS + tt, C), F32), pltpu.VMEM((tt, C), F32)],
        compiler_params=_cp("arbitrary", "arbitrary"),
    )(hist_arr, u, conv_w, conv_b.reshape(-1, 1, C), ln_g.reshape(-1, 1, C), ln_b.reshape(-1, 1, C))


def _lam(lq1, lk1, lq2, lk2, lam_init):
    a = jnp.exp(jnp.sum(lq1[...] * lk1[...], axis=-1, keepdims=True))
    b = jnp.exp(jnp.sum(lq2[...] * lk2[...], axis=-1, keepdims=True))
    return a - b + lam_init


def _subln(o, g, lam_init):
    on = o * lax.rsqrt(jnp.mean(o * o, axis=-1, keepdims=True) + 1e-6) * g
    return on * (1.0 - lam_init)


def _diff_prompt_kernel(q_ref, k_ref, v_ref, lq1, lk1, lq2, lk2, gs_ref, o_ref, acc_ref, *, tq, tk, hp, lam_init):
    qi = pl.program_id(2)
    hw = 2 * HEAD_DIM
    q = q_ref[...]
    acc_ref[...] = jnp.zeros_like(acc_ref)
    row = qi * tq + _iota((tq, tk), 0)
    col0 = _iota((tq, tk), 1)

    def make_body(masked):
        def body(kc, carry):
            k0 = pl.multiple_of(kc * tk, tk)
            kblk = k_ref[pl.ds(k0, tk), :]
            vblk = v_ref[pl.ds(k0, tk), :]
            if masked:
                mask = (k0 + col0) <= row
            new = []
            for idx in range(2 * hp):
                lo = idx * HEAD_DIM
                m_old, l_old = carry[2 * idx], carry[2 * idx + 1]
                s = lax.dot_general(q[:, lo:lo + HEAD_DIM], kblk[:, lo:lo + HEAD_DIM], NT_DIMS,
                                    preferred_element_type=F32)
                if masked:
                    s = jnp.where(mask, s, NEG)
                m_new = jnp.maximum(m_old, jnp.max(s, axis=-1, keepdims=True))
                p = jnp.exp(s - m_new)
                alpha = jnp.exp(m_old - m_new)
                vh = vblk[:, (idx // 2) * hw:(idx // 2 + 1) * hw]
                acc_ref[idx] = alpha * acc_ref[idx] + jnp.dot(p.astype(BF16), vh, preferred_element_type=F32)
                new += [m_new, alpha * l_old + jnp.sum(p, axis=-1, keepdims=True)]
            return tuple(new)
        return body

    init = (jnp.full((tq, 1), NEG, F32), jnp.zeros((tq, 1), F32)) * (2 * hp)
    n_full = (qi * tq) // tk
    carry = lax.fori_loop(0, n_full, make_body(False), init)
    carry = make_body(True)(n_full, carry)
    lam = _lam(lq1, lk1, lq2, lk2, lam_init)
    for hh in range(hp):
        o = acc_ref[2 * hh] / carry[4 * hh + 1] - lam * (acc_ref[2 * hh + 1] / carry[4 * hh + 3])
        o_ref[:, hh * hw:(hh + 1) * hw] = _subln(o, gs_ref[...], lam_init).astype(o_ref.dtype)


def _diff_prompt(dq, dk, dv, lam_params, g_subln, l, S, T, lam_init):
    M, DW = dq.shape
    hw = 2 * HEAD_DIM
    H = DW // hw
    hp = 2 if H % 2 == 0 else 1
    tq = _pick(T, (256, 128))
    tk = 2 * tq
    assert T % tk == 0
    nq = T // tq
    vec = pl.BlockSpec((None, 1, HEAD_DIM), lambda s, h, i: (l, 0, 0))
    return pl.pallas_call(
        functools.partial(_diff_prompt_kernel, tq=tq, tk=tk, hp=hp, lam_init=lam_init),
        grid=(S, H // hp, nq),
        in_specs=[pl.BlockSpec((tq, hp * hw), lambda s, h, i: (s * nq + i, h)),
                  pl.BlockSpec((T, hp * hw), lambda s, h, i: (s, h)),
                  pl.BlockSpec((T, hp * hw), lambda s, h, i: (s, h)),
                  vec, vec, vec, vec,
                  pl.BlockSpec((None, 1, hw), lambda s, h, i: (l, 0, 0))],
        out_specs=pl.BlockSpec((tq, hp * hw), lambda s, h, i: (s * nq + i, h)),
        out_shape=jax.ShapeDtypeStruct((M, DW), BF16),
        scratch_shapes=[pltpu.VMEM((2 * hp, tq, hw), F32)],
        compiler_params=_cp("arbitrary", "arbitrary", "arbitrary"),
    )(dq, dk, dv, *[p.reshape(-1, 1, HEAD_DIM) for p in lam_params], g_subln.reshape(-1, 1, hw))


def _diff_sample_kernel(pt_ref, q_ref, kc_ref, vc_ref, kn_ref, vn_ref, lq1, lk1, lq2, lk2, gs_ref, o_ref,
                        m_ref, l_ref, acc_ref, *, n_pages, n_new, lam_init):
    p = pl.program_id(1)
    n_heads = q_ref.shape[0]
    rph = 2 * SUBLANE

    @pl.when(p == 0)
    def _():
        m_ref[...] = jnp.full_like(m_ref, NEG)
        l_ref[...] = jnp.zeros_like(l_ref)
        acc_ref[...] = jnp.zeros_like(acc_ref)

    def step(k_ref, v_ref, valid):
        s = jnp.concatenate(
            [lax.dot_general(q_ref[h], k_ref[:, h, :].astype(BF16), NT_DIMS, preferred_element_type=F32)
             for h in range(n_heads)], axis=0)
        if valid is not None:
            s = jnp.where(valid, s, NEG)
        m_old = m_ref[...]
        m_new = jnp.maximum(m_old, jnp.max(s, axis=-1, keepdims=True))
        pr = jnp.exp(s - m_new)
        alpha = jnp.exp(m_old - m_new)
        l_ref[...] = alpha * l_ref[...] + jnp.sum(pr, axis=-1, keepdims=True)
        pb = pr.astype(BF16)
        pv = jnp.concatenate(
            [jnp.dot(pb[h * rph:(h + 1) * rph], v_ref[:, h, :].astype(BF16), preferred_element_type=F32)
             for h in range(n_heads)], axis=0)
        acc_ref[...] = alpha * acc_ref[...] + pv
        m_ref[...] = m_new

    @pl.when(p < n_pages)
    def _():
        step(kc_ref, vc_ref, None)

    @pl.when(p == n_pages)
    def _():
        R, nk = n_heads * rph, kn_ref.shape[0]
        j = _iota((R, nk), 1)
        t = _vmod(_iota((R, nk), 0), SUBLANE)
        step(kn_ref, vn_ref, (j < n_new) & (j <= t))
        lam = _lam(lq1, lk1, lq2, lk2, lam_init)
        on = acc_ref[...] / l_ref[...]
        o = jnp.concatenate([on[h * rph:h * rph + SUBLANE] - lam * on[h * rph + SUBLANE:(h + 1) * rph]
                             for h in range(n_heads)], axis=0)
        o_ref[...] = _subln(o, gs_ref[...], lam_init)


def _diff_sample(qbd, cache_k, cache_v, k_new, v_new, page_table, lam_params, g_subln, l, n_new, lam_init):
    S, H, rph, hw = qbd.shape
    n_pages = page_table.shape[1]
    page = cache_k.shape[2]
    nr = k_new.shape[1]
    vec = pl.BlockSpec((None, 1, HEAD_DIM), lambda s, p, pt: (l, 0, 0))
    cache_spec = pl.BlockSpec((None, None, page, H, hw),
                              lambda s, p, pt: (l, pt[s, jnp.minimum(p, n_pages - 1)], 0, 0, 0))
    new_spec = pl.BlockSpec((None, nr, H, hw), lambda s, p, pt: (s, 0, 0, 0))
    grid_spec = pltpu.PrefetchScalarGridSpec(
        num_scalar_prefetch=1,
        grid=(S, n_pages + 1),
        in_specs=[pl.BlockSpec((None, H, rph, hw), lambda s, p, pt: (s, 0, 0, 0)),
                  cache_spec, cache_spec, new_spec, new_spec,
                  vec, vec, vec, vec,
                  pl.BlockSpec((None, 1, hw), lambda s, p, pt: (l, 0, 0))],
        out_specs=pl.BlockSpec((None, H * SUBLANE, hw), lambda s, p, pt: (s, 0, 0)),
        scratch_shapes=[pltpu.VMEM((H * rph, 1), F32), pltpu.VMEM((H * rph, 1), F32),
                        pltpu.VMEM((H * rph, hw), F32)],
    )
    return pl.pallas_call(
        functools.partial(_diff_sample_kernel, n_pages=n_pages, n_new=n_new, lam_init=lam_init),
        grid_spec=grid_spec,
        out_shape=jax.ShapeDtypeStruct((S, H * SUBLANE, hw), F32),
        compiler_params=_cp("arbitrary", "arbitrary"),
    )(page_table, qbd, cache_k, cache_v, k_new, v_new,
      *[p.reshape(-1, 1, HEAD_DIM) for p in lam_params], g_subln.reshape(-1, 1, hw))


def _compress_prompt_kernel(k_ref, v_ref, wk_ref, wv_ref, ok_ref, ov_ref):
    for src, w_ref, o_ref in ((k_ref, wk_ref, ok_ref), (v_ref, wv_ref, ov_ref)):
        w = w_ref[...]
        for n in range(src.shape[0] // NSA_BLOCK):
            o_ref[n:n + 1, :] = jnp.sum(src[n * NSA_BLOCK:(n + 1) * NSA_BLOCK, :] * w, axis=0, keepdims=True)


def _compress_prompt(kv, wc_k, wc_v, l):
    M = kv.shape[0]
    kvw = wc_k.shape[-1]
    rows = SUBLANE * NSA_BLOCK
    w_spec = pl.BlockSpec((None, NSA_BLOCK, kvw), lambda i: (l, 0, 0))
    o_spec = pl.BlockSpec((SUBLANE, kvw), lambda i: (i, 0))
    o_shape = jax.ShapeDtypeStruct((M // NSA_BLOCK, kvw), F32)
    return pl.pallas_call(
        _compress_prompt_kernel,
        grid=(M // rows,),
        in_specs=[pl.BlockSpec((rows, kvw), lambda i: (i, 0)), pl.BlockSpec((rows, kvw), lambda i: (i, 1)),
                  w_spec, w_spec],
        out_specs=[o_spec, o_spec], out_shape=[o_shape, o_shape],
        compiler_params=_cp("arbitrary"),
    )(kv, kv, wc_k, wc_v)


def _compress_sample_kernel(pt_ref, *refs, n_src):
    k_srcs = refs[:n_src]
    v_srcs = refs[n_src:2 * n_src]
    wk_ref, wv_ref, ok_ref, ov_ref = refs[2 * n_src:]
    for srcs, w_ref, o_ref in ((k_srcs, wk_ref, ok_ref), (v_srcs, wv_ref, ov_ref)):
        for g in range(G_NSA):
            w = w_ref[:, g, :]
            n = 0
            for src in srcs:
                for b0 in range(0, src.shape[0], NSA_BLOCK):
                    o_ref[g, n:n + 1, :] = jnp.sum(src[b0:b0 + NSA_BLOCK, g, :] * w, axis=0, keepdims=True)
                    n += 1


def _compress_sample(cache_k, cache_v, page_table, wc_k, wc_v, l):
    S, n_pages = page_table.shape
    page = cache_k.shape[2]
    per_step = (SUBLANE * NSA_BLOCK) // page
    assert n_pages % per_step == 0
    nblk = n_pages * page // NSA_BLOCK

    def cspec(i):
        return pl.BlockSpec((None, None, page, G_NSA, HEAD_DIM),
                            lambda s, t, pt: (l, pt[s, t * per_step + i], 0, 0, 0))
    w_spec = pl.BlockSpec((None, NSA_BLOCK, G_NSA, HEAD_DIM), lambda s, t, pt: (l, 0, 0, 0))
    o_spec = pl.BlockSpec((None, G_NSA, SUBLANE, HEAD_DIM), lambda s, t, pt: (s, 0, t, 0))
    o_shape = jax.ShapeDtypeStruct((S, G_NSA, nblk, HEAD_DIM), F32)
    grid_spec = pltpu.PrefetchScalarGridSpec(
        num_scalar_prefetch=1, grid=(S, n_pages // per_step),
        in_specs=[cspec(i) for i in range(per_step)] * 2 + [w_spec, w_spec],
        out_specs=[o_spec, o_spec])
    return pl.pallas_call(
        functools.partial(_compress_sample_kernel, n_src=per_step),
        grid_spec=grid_spec, out_shape=[o_shape, o_shape],
        compiler_params=_cp("arbitrary", "arbitrary"),
    )(page_table, *([cache_k] * per_step), *([cache_v] * per_step), wc_k, wc_v)


def _rank_select(score, n_keep):
    R, N = score.shape
    n_i = _iota((R, N), 1)
    rank = jnp.zeros((R, N), F32)
    for m in range(N):
        col = score[:, m:m + 1]
        beats = jnp.where(col > score, 1.0, jnp.where(col == score, jnp.where(n_i > m, 1.0, 0.0), 0.0))
        rank = rank + beats
    return jnp.where(rank < n_keep, 1.0, 0.0)


def _softmax_rows(s):
    m = jnp.max(s, axis=-1, keepdims=True)
    e = jnp.exp(s - m)
    return e / jnp.sum(e, axis=-1, keepdims=True)


def _nsa_prompt_kernel(q_ref, kcb_ref, vcb_ref, sk_ref, sv_ref, wk_ref, wv_ref, g_ref, o_ref, acc_ref, *,
                       tq, tk, hg):
    qi = pl.program_id(2)
    nblk = kcb_ref.shape[0]
    R = hg * tq
    q = jnp.concatenate([q_ref[:, hh * HEAD_DIM:(hh + 1) * HEAD_DIM] for hh in range(hg)], axis=0)

    kcb = kcb_ref[...].astype(BF16)
    vcb = vcb_ref[...].astype(BF16)
    t_rn = qi * tq + _vmod(_iota((R, nblk), 0), tq)
    n_rn = _iota((R, nblk), 1)
    complete = (n_rn + 1) * NSA_BLOCK <= t_rn + 1
    s_c = lax.dot_general(q, kcb, NT_DIMS, preferred_element_type=F32)
    p_c = jnp.where(complete, _softmax_rows(jnp.where(complete, s_c, NEG)), 0.0)
    o_c = jnp.dot(p_c.astype(BF16), vcb, preferred_element_type=F32)

    imp = p_c[0:tq]
    for hh in range(1, hg):
        imp = imp + p_c[hh * tq:(hh + 1) * tq]
    t_tn = qi * tq + _iota((tq, nblk), 0)
    n_tn = _iota((tq, nblk), 1)
    complete_t = (n_tn + 1) * NSA_BLOCK <= t_tn + 1
    score = jnp.where(n_tn == _vdiv(t_tn, NSA_BLOCK), FORCE, jnp.where(complete_t, imp, -1.0))
    sel = _rank_select(score, min(NSA_TOPK, nblk)) * jnp.where(score >= 0.0, 1.0, 0.0)
    bias_r = jnp.concatenate([(sel - 1.0) * (-NEG)] * hg, axis=0).astype(BF16)

    acc_ref[...] = jnp.zeros_like(acc_ref)
    t_rk = qi * tq + _vmod(_iota((R, tk), 0), tq)
    j_rk = _iota((R, tk), 1)

    def make_body(masked):
        def body(kc, carry):
            m_old, l_old = carry
            k0 = pl.multiple_of(kc * tk, tk)
            blk = _vdiv(k0 + _iota((nblk, tk), 1), NSA_BLOCK)
            expand = jnp.where(_iota((nblk, tk), 0) == blk, 1.0, 0.0).astype(BF16)
            s = (lax.dot_general(q, sk_ref[pl.ds(k0, tk), :], NT_DIMS, preferred_element_type=F32)
                 + jnp.dot(bias_r, expand, preferred_element_type=F32))
            if masked:
                s = jnp.where(k0 + j_rk <= t_rk, s, NEG)
            m_new = jnp.maximum(m_old, jnp.max(s, axis=-1, keepdims=True))
            p = jnp.exp(s - m_new)
            alpha = jnp.exp(m_old - m_new)
            acc_ref[...] = alpha * acc_ref[...] + jnp.dot(p.astype(BF16), sv_ref[pl.ds(k0, tk), :],
                                                          preferred_element_type=F32)
            return m_new, alpha * l_old + jnp.sum(p, axis=-1, keepdims=True)
        return body

    n_full = (qi * tq) // tk
    carry = lax.fori_loop(0, n_full, make_body(False), (jnp.full((R, 1), NEG, F32), jnp.zeros((R, 1), F32)))
    _, l_s = make_body(True)(n_full, carry)
    o_s = acc_ref[...] / l_s

    wlen = WINDOW + tq
    w0 = pl.multiple_of(jnp.maximum(qi * tq - WINDOW, 0), tq)
    kpos = w0 + _iota((R, wlen), 1)
    t_rw = qi * tq + _vmod(_iota((R, wlen), 0), tq)
    valid = (kpos <= t_rw) & (kpos > t_rw - WINDOW)
    s_w = lax.dot_general(q, wk_ref[pl.ds(w0, wlen), :], NT_DIMS, preferred_element_type=F32)
    s_w = jnp.where(valid, s_w, NEG)
    p_w = jnp.exp(s_w - jnp.max(s_w, axis=-1, keepdims=True))
    o_w = (jnp.dot(p_w.astype(BF16), wv_ref[pl.ds(w0, wlen), :], preferred_element_type=F32)
           / jnp.sum(p_w, axis=-1, keepdims=True))

    g = g_ref[...]
    for hh in range(hg):
        rs = slice(hh * tq, (hh + 1) * tq)
        o = (g[:, 3 * hh:3 * hh + 1] * o_c[rs] + g[:, 3 * hh + 1:3 * hh + 2] * o_s[rs]
             + g[:, 3 * hh + 2:3 * hh + 3] * o_w[rs])
        o_ref[:, hh * HEAD_DIM:(hh + 1) * HEAD_DIM] = o.astype(o_ref.dtype)


def _nsa_prompt(nq, kcb, vcb, kvb, gates, S, T, hg):
    M, QW = nq.shape
    tq = _pick(T, (128, 64))
    tk = 4 * tq
    assert T % tk == 0 and T >= WINDOW + tq
    nqt = T // tq
    nblk = T // NSA_BLOCK
    gw = hg * HEAD_DIM

    def kv_spec(seg):
        return pl.BlockSpec((T, HEAD_DIM), lambda s, g, i: (s, seg * G_NSA + g))
    cb_spec = pl.BlockSpec((nblk, HEAD_DIM), lambda s, g, i: (s, g))
    return pl.pallas_call(
        functools.partial(_nsa_prompt_kernel, tq=tq, tk=tk, hg=hg),
        grid=(S, G_NSA, nqt),
        in_specs=[pl.BlockSpec((tq, gw), lambda s, g, i: (s * nqt + i, g)),
                  cb_spec, cb_spec, kv_spec(2), kv_spec(3), kv_spec(4), kv_spec(5),
                  pl.BlockSpec((tq, LANE), lambda s, g, i: (s * nqt + i, g))],
        out_specs=pl.BlockSpec((tq, gw), lambda s, g, i: (s * nqt + i, g)),
        out_shape=jax.ShapeDtypeStruct((M, QW), BF16),
        scratch_shapes=[pltpu.VMEM((hg * tq, HEAD_DIM), F32)],
        compiler_params=_cp("arbitrary", "arbitrary", "arbitrary"),
    )(nq, kcb, vcb, kvb, kvb, kvb, kvb, gates)


def _nsa_sample_kernel(pt_ref, q_ref, kcb_ref, vcb_ref, sk_ref, sv_ref, skn_ref, svn_ref, wkn_ref, wvn_ref,
                       wk_ref, wv_ref, g_ref, o_ref, bias_ref, oc_ref, m_ref, l_ref, acc_ref, *,
                       n_pages, n_new, hg):
    p = pl.program_id(1)
    rpg = hg * SUBLANE
    R = G_NSA * rpg
    page = sk_ref.shape[0]
    nblk_past = kcb_ref.shape[1]
    bpp = page // NSA_BLOCK

    def per_group(fn):
        return jnp.concatenate([fn(g) for g in range(G_NSA)], axis=0)

    def qk(k_ref):
        return per_group(lambda g: lax.dot_general(q_ref[g], k_ref[:, g, :].astype(BF16), NT_DIMS,
                                                   preferred_element_type=F32))

    def pv(pr, v_ref):
        pb = pr.astype(BF16)
        return per_group(lambda g: jnp.dot(pb[g * rpg:(g + 1) * rpg], v_ref[:, g, :].astype(BF16),
                                           preferred_element_type=F32))

    @pl.when(p == 0)
    def _():
        p_c = _softmax_rows(per_group(lambda g: lax.dot_general(
            q_ref[g], kcb_ref[g].astype(BF16), NT_DIMS, preferred_element_type=F32)))
        pcb = p_c.astype(BF16)
        oc_ref[...] = per_group(lambda g: jnp.dot(pcb[g * rpg:(g + 1) * rpg], vcb_ref[g].astype(BF16),
                                                  preferred_element_type=F32))

        def group_bias(g):
            imp = p_c[g * rpg:g * rpg + SUBLANE]
            for hh in range(1, hg):
                imp = imp + p_c[g * rpg + hh * SUBLANE:g * rpg + (hh + 1) * SUBLANE]
            sel = _rank_select(imp, min(NSA_TOPK, nblk_past + 1) - 1)
            return jnp.concatenate([(sel - 1.0) * (-NEG)] * hg, axis=0)
        bias_ref[...] = per_group(group_bias).astype(BF16)
        m_ref[...] = jnp.full_like(m_ref, NEG)
        l_ref[...] = jnp.zeros_like(l_ref)
        acc_ref[...] = jnp.zeros_like(acc_ref)

    def step(s, v_ref):
        m_old = m_ref[...]
        m_new = jnp.maximum(m_old, jnp.max(s, axis=-1, keepdims=True))
        pr = jnp.exp(s - m_new)
        alpha = jnp.exp(m_old - m_new)
        l_ref[...] = alpha * l_ref[...] + jnp.sum(pr, axis=-1, keepdims=True)
        acc_ref[...] = alpha * acc_ref[...] + pv(pr, v_ref)
        m_ref[...] = m_new

    blk = p * bpp + _vdiv(_iota((nblk_past, page), 1), NSA_BLOCK)
    expand = jnp.where(_iota((nblk_past, page), 0) == blk, 1.0, 0.0).astype(BF16)
    step(qk(sk_ref) + jnp.dot(bias_ref[...], expand, preferred_element_type=F32), sv_ref)

    @pl.when(p == n_pages - 1)
    def _():
        nr = skn_ref.shape[0]
        new_valid = (_iota((R, nr), 1) < n_new) & (_iota((R, nr), 1) <= _vmod(_iota((R, nr), 0), SUBLANE))
        step(jnp.where(new_valid, qk(skn_ref), NEG), svn_ref)
        o_s = acc_ref[...] / l_ref[...]
        wb = wk_ref.shape[0]
        buf_valid = _iota((R, wb), 1) > _vmod(_iota((R, wb), 0), SUBLANE) + (wb - WINDOW)
        s_b = jnp.where(buf_valid, qk(wk_ref), NEG)
        s_n = jnp.where(new_valid, qk(wkn_ref), NEG)
        m = jnp.maximum(jnp.max(s_b, axis=-1, keepdims=True), jnp.max(s_n, axis=-1, keepdims=True))
        p_b = jnp.exp(s_b - m)
        p_n = jnp.exp(s_n - m)
        lw = jnp.sum(p_b, axis=-1, keepdims=True) + jnp.sum(p_n, axis=-1, keepdims=True)
        o_w = (pv(p_b, wv_ref) + pv(p_n, wvn_ref)) / lw
        g = g_ref[...]
        o_ref[...] = g[:, 0:1] * oc_ref[...] + g[:, 1:2] * o_s + g[:, 2:3] * o_w


def _nsa_sample(q, kcb, vcb, cache_sk, cache_sv, sk_new, sv_new, wk_new, wv_new, win_k, win_v, gates_r,
                page_table, l, n_new, hg):
    S = q.shape[0]
    rpg = q.shape[2]
    R = G_NSA * rpg
    n_pages = page_table.shape[1]
    page = cache_sk.shape[2]
    nblk_past = kcb.shape[2]
    wb = win_k.shape[2]
    nr = sk_new.shape[1]

    def per_s(*shape):
        return pl.BlockSpec((None,) + shape, lambda s, p, pt: (s,) + (0,) * len(shape))
    cache_spec = pl.BlockSpec((None, None, page, G_NSA, HEAD_DIM), lambda s, p, pt: (l, pt[s, p], 0, 0, 0))
    win_spec = pl.BlockSpec((None, None, wb, G_NSA, HEAD_DIM), lambda s, p, pt: (l, s, 0, 0, 0))
    new_spec = per_s(nr, G_NSA, HEAD_DIM)
    cb_spec = per_s(G_NSA, nblk_past, HEAD_DIM)
    grid_spec = pltpu.PrefetchScalarGridSpec(
        num_scalar_prefetch=1, grid=(S, n_pages),
        in_specs=[per_s(G_NSA, rpg, HEAD_DIM), cb_spec, cb_spec, cache_spec, cache_spec,
                  new_spec, new_spec, new_spec, new_spec, win_spec, win_spec, per_s(R, LANE)],
        out_specs=per_s(R, HEAD_DIM),
        scratch_shapes=[pltpu.VMEM((R, nblk_past), BF16), pltpu.VMEM((R, HEAD_DIM), F32),
                        pltpu.VMEM((R, 1), F32), pltpu.VMEM((R, 1), F32), pltpu.VMEM((R, HEAD_DIM), F32)])
    return pl.pallas_call(
        functools.partial(_nsa_sample_kernel, n_pages=n_pages, n_new=n_new, hg=hg),
        grid_spec=grid_spec,
        out_shape=jax.ShapeDtypeStruct((S, R, HEAD_DIM), F32),
        compiler_params=_cp("arbitrary", "arbitrary"),
    )(page_table, q, kcb, vcb, cache_sk, cache_sv, sk_new, sv_new, wk_new, wv_new, win_k, win_v, gates_r)


def kernel(x_prompt, x_sample, cache_diff_k, cache_diff_v, cache_cmp_k, cache_cmp_v, cache_sel_k, cache_sel_v, state_win_k, state_win_v, state_conv, page_table, c_prompt, c_sample, w_ada, b_ada, g_pre_mix, g_post_mix, g_pre_ffn, g_post_ffn, w_in, conv_w, conv_b, conv_ln_g, conv_ln_b, w_conv_out, lam_q1, lam_k1, lam_q2, lam_k2, g_subln, w_diff_out, w_cmp_k, w_cmp_v, w_nsa_out, w_out, w_gate_up, w_down):
    B, T, D = x_prompt.shape
    Bs, Ts, _ = x_sample.shape
    depth = w_in.shape[0]
    C = conv_w.shape[2]
    DW = w_diff_out.shape[1]
    H = DW // (2 * HEAD_DIM)
    QW = w_nsa_out.shape[1]
    hg = QW // (HEAD_DIM * G_NSA)
    KVW = G_NSA * HEAD_DIM
    n_gate = 3 * G_NSA * hg
    d_ff = w_down.shape[1]
    cw = conv_w.shape[1]
    off_dq = 2 * C
    off_dk = off_dq + DW
    off_dv = off_dk + DW
    off_nq = off_dv + DW
    off_kv = off_nq + QW
    off_ng = off_kv + 6 * KVW
    off_mg = off_ng + n_gate
    assert off_mg + 3 * D == w_in.shape[2] and Ts <= SUBLANE

    w_ng = w_in[:, :, off_ng:off_mg].reshape(depth, D, G_NSA, 3 * hg)
    w_ng = jnp.pad(w_ng, ((0, 0), (0, 0), (0, 0), (0, LANE - 3 * hg))).reshape(depth, D, G_NSA * LANE)
    w_down_b = _cast_bf16(w_down)
    wc_k2 = w_cmp_k.reshape(depth, NSA_BLOCK, KVW)
    wc_v2 = w_cmp_v.reshape(depth, NSA_BLOCK, KVW)

    n_seq = B + Bs
    c_all = jnp.pad(jnp.concatenate([c_prompt, c_sample], axis=0), ((0, -n_seq % SUBLANE), (0, 0)))
    mod = _ada(c_all, w_ada, b_ada)
    mod_p = mod[:, :B].reshape(depth, B, 1, 6 * D)
    mod_s = jnp.repeat(mod[:, B:n_seq], Ts, axis=1).reshape(depth, 1, Bs * Ts, 6 * D)

    Mp, Ms = B * T, Bs * Ts
    xp = x_prompt
    xs = x_sample.reshape(1, Ms, D)
    (hp,) = _norm(xp, mod_p, g_pre=g_pre_mix, l_pre=0, sc_chunk=1, sh_chunk=0)
    (hs,) = _norm(xs, mod_s, g_pre=g_pre_mix, l_pre=0, sc_chunk=1, sh_chunk=0)

    eye_2 = jnp.eye(2, dtype=BF16)
    outs = {n: [] for n in ("p_dk", "p_dv", "p_ck", "p_cv", "p_sk", "p_sv", "p_wk", "p_wv", "p_conv",
                            "s_dk", "s_dv", "s_ck", "s_cv", "s_sk", "s_sv", "s_wk", "s_wv", "s_conv")}

    for l in range(depth):
        lam_init = 0.8 - 0.6 * math.exp(-0.3 * l)
        lam_params = (lam_q1, lam_k1, lam_q2, lam_k2)

        def project(h2d):
            wide = dict(tm=512, tn=1024)
            u = _mm(h2d, w_in, l, 0, C, out_dtype=F32, epilogue="glu", pair_off=C, tm=512, tn=512)
            dq = _mm(h2d, w_in, l, off_dq, DW, out_dtype=BF16, out_scale=QK_SCALE, **wide)
            dk, dkb = _mm(h2d, w_in, l, off_dk, DW, out_dtype=F32, bf16_copy=True, **wide)
            dv, dvb = _mm(h2d, w_in, l, off_dv, DW, out_dtype=F32, bf16_copy=True, **wide)
            nq = _mm(h2d, w_in, l, off_nq, QW, out_dtype=BF16, out_scale=QK_SCALE, **wide)
            kv, kvb = _mm(h2d, w_in, l, off_kv, 6 * KVW, out_dtype=F32, bf16_copy=True, **wide)
            sg = _mm(h2d, w_ng, l, 0, G_NSA * LANE, out_dtype=F32, epilogue="sigmoid")
            smg = _mm(h2d, w_in, l, off_mg, 3 * D, out_dtype=BF16, epilogue="sigmoid", **wide)
            return u, dq, dk, dkb, dv, dvb, nq, kv, kvb, sg, smg

        def finish(x3, mod_g, a_conv, a_diff, a_nsa, smg):
            merged = _merge_mm(a_conv, a_diff, a_nsa, w_conv_out, w_diff_out, w_nsa_out, smg, l)
            y = _mm(merged, w_out, l, 0, D, out_dtype=F32, tm=512, tn=1024)
            x1, h2 = _norm(x3, mod_g, y=y.reshape(x3.shape), g_post=g_post_mix, l_post=l, gate_chunk=2,
                           g_pre=g_pre_ffn, l_pre=l, sc_chunk=4, sh_chunk=3)
            act = _mm(h2.reshape(-1, D), w_gate_up, l, 0, d_ff, out_dtype=BF16, epilogue="swiglu",
                      pair_off=d_ff, tn=256)
            f = _mm(act, w_down_b, l, 0, D, out_dtype=F32)
            if l + 1 < depth:
                return _norm(x1, mod_g, y=f.reshape(x3.shape), g_post=g_post_ffn, l_post=l, gate_chunk=5,
                             g_pre=g_pre_mix, l_pre=l + 1, sc_chunk=1, sh_chunk=0)
            (x2,) = _norm(x1, mod_g, y=f.reshape(x3.shape), g_post=g_post_ffn, l_post=l, gate_chunk=5)
            return x2, None

        u, dq, dk, dkb, dv, dvb, nq, kv, kvb, sg, smg = project(hp.reshape(Mp, D))
        a_conv = _conv_module(u.reshape(B, T, C), None, conv_w, conv_b, conv_ln_g, conv_ln_b, l)
        a_diff = _diff_prompt(dq, dkb, dvb, lam_params, g_subln, l, B, T, lam_init)
        kcb, vcb = _compress_prompt(kv, wc_k2, wc_v2, l)
        a_nsa = _nsa_prompt(nq, kcb, vcb, kvb, sg, B, T, hg)
        xp, hp = finish(xp, mod_p, a_conv.reshape(Mp, C), a_diff, a_nsa, smg)
        kv5 = kv.reshape(B, T, 6, G_NSA, HEAD_DIM)
        keep = min(WINDOW, T)
        outs["p_dk"].append(dk.reshape(B, T, H, 2 * HEAD_DIM))
        outs["p_dv"].append(dv.reshape(B, T, H, 2 * HEAD_DIM))
        for i, n in enumerate(("p_ck", "p_cv", "p_sk", "p_sv")):
            outs[n].append(kv5[:, :, i])
        outs["p_wk"].append(kv5[:, T - keep:, 4])
        outs["p_wv"].append(kv5[:, T - keep:, 5])
        outs["p_conv"].append(u.reshape(B, T, C)[:, T - (cw - 1):])

        u, dq, dk, _, dv, _, nq, kv, _, sg, smg = project(hs.reshape(Ms, D))
        u3 = u.reshape(Bs, Ts, C)
        hist = jnp.pad(state_conv[l], ((0, 0), (HIST_ROWS - (cw - 1), 0), (0, 0)))
        a_conv = _conv_module(jnp.pad(u3, ((0, 0), (0, 16 - Ts), (0, 0))), hist,
                              conv_w, conv_b, conv_ln_g, conv_ln_b, l)[:, :Ts].reshape(Ms, C)

        def new_rows(a, *tail):
            return jnp.pad(a.reshape((Bs, Ts) + tail), ((0, 0), (0, SUBLANE - Ts)) + ((0, 0),) * len(tail))

        q5 = new_rows(dq, H, 2, HEAD_DIM)
        qbd = jnp.einsum("bthcd,cj->bhctjd", q5, eye_2).reshape(Bs, H, 2 * SUBLANE, 2 * HEAD_DIM)
        o_d = _diff_sample(qbd, cache_diff_k, cache_diff_v, new_rows(dk, H, 2 * HEAD_DIM),
                           new_rows(dv, H, 2 * HEAD_DIM), page_table, lam_params, g_subln, l, Ts, lam_init)
        a_diff = (o_d.reshape(Bs, H, SUBLANE, 2 * HEAD_DIM)[:, :, :Ts].transpose(0, 2, 1, 3)
                  .reshape(Ms, DW).astype(BF16))

        kv5 = kv.reshape(Bs, Ts, 6, G_NSA, HEAD_DIM)
        new6 = [new_rows(kv[:, i * KVW:(i + 1) * KVW], G_NSA, HEAD_DIM) for i in range(6)]
        kcb, vcb = _compress_sample(cache_cmp_k, cache_cmp_v, page_table, w_cmp_k, w_cmp_v, l)
        qn = new_rows(nq, G_NSA, hg, HEAD_DIM).transpose(0, 2, 3, 1, 4).reshape(Bs, G_NSA, hg * SUBLANE, HEAD_DIM)
        gr = new_rows(sg, G_NSA, LANE)[..., :3 * hg].reshape(Bs, SUBLANE, G_NSA, hg, 3)
        gr = jnp.pad(gr.transpose(0, 2, 3, 1, 4).reshape(Bs, G_NSA * hg * SUBLANE, 3),
                     ((0, 0), (0, 0), (0, LANE - 3)))
        o_n = _nsa_sample(qn, kcb, vcb, cache_sel_k, cache_sel_v, new6[2], new6[3], new6[4], new6[5],
                          state_win_k, state_win_v, gr, page_table, l, Ts, hg)
        a_nsa = (o_n.reshape(Bs, G_NSA, hg, SUBLANE, HEAD_DIM)[:, :, :, :Ts].transpose(0, 3, 1, 2, 4)
                 .reshape(Ms, QW).astype(BF16))
        xs, hs = finish(xs, mod_s, a_conv, a_diff, a_nsa, smg)
        outs["s_dk"].append(dk.reshape(Bs, Ts, H, 2 * HEAD_DIM))
        outs["s_dv"].append(dv.reshape(Bs, Ts, H, 2 * HEAD_DIM))
        for i, n in enumerate(("s_ck", "s_cv", "s_sk", "s_sv")):
            outs[n].append(kv5[:, :, i])
        outs["s_wk"].append(jnp.concatenate([state_win_k[l], kv5[:, :, 4]], axis=1)[:, Ts:])
        outs["s_wv"].append(jnp.concatenate([state_win_v[l], kv5[:, :, 5]], axis=1)[:, Ts:])
        outs["s_conv"].append(jnp.concatenate([state_conv[l], u3], axis=1)[:, Ts:])

    st = {n: jnp.stack(v) for n, v in outs.items()}
    return (xp, xs.reshape(Bs, Ts, D),
            st["p_dk"], st["p_dv"], st["p_ck"], st["p_cv"], st["p_sk"], st["p_sv"], st["p_wk"], st["p_wv"],
            st["p_conv"],
            st["s_dk"], st["s_dv"], st["s_ck"], st["s_cv"], st["s_sk"], st["s_sv"], st["s_wk"], st["s_wv"],
            st["s_conv"])
```

```python
import functools
import math

import jax
import jax.numpy as jnp
from jax import lax
from jax.experimental import pallas as pl
from jax.experimental.pallas import tpu as pltpu

F32 = jnp.float32
BF16 = jnp.bfloat16

HEAD_DIM = 128
G_NSA = 4
NSA_BLOCK = 64
NSA_TOPK = 16
WINDOW = 512
NEG = -1e30
MASKED = 2 * NEG
FORCE = 1e9
LANE = 128
SUBLANE = 8
V7X_VMEM_LIMIT = 60 * 1024 * 1024
NT_DIMS = (((1,), (1,)), ((), ()))
QK_SCALE = HEAD_DIM ** -0.5


def _cp(*sem):
    return pltpu.CompilerParams(dimension_semantics=sem, vmem_limit_bytes=V7X_VMEM_LIMIT)


def _pick(n, cands):
    for c in cands:
        if n % c == 0:
            return c
    return n


def _sigmoid(x):
    return 1.0 / (1.0 + jnp.exp(-x))


def _vdiv(x, n):
    assert n & (n - 1) == 0
    return lax.shift_right_logical(x, jnp.int32(n.bit_length() - 1))


def _vmod(x, n):
    assert n & (n - 1) == 0
    return lax.bitwise_and(x, jnp.int32(n - 1))


def _iota(shape, dim):
    return lax.broadcasted_iota(jnp.int32, shape, dim)


def _cast_weights(w_refs, wn_refs, wb_refs, cast_rows, shift):
    k_dim, tn = wb_refs[0].shape

    def body(c, carry):
        r0 = pl.multiple_of(c * cast_rows, cast_rows)
        for i, (w, wb) in enumerate(zip(w_refs, wb_refs)):
            wv = w[pl.ds(r0, cast_rows), :]
            if shift:
                wv = jnp.concatenate([wv, wn_refs[i][pl.ds(r0, cast_rows), :]], axis=1)[:, shift:shift + tn]
            wb[pl.ds(r0, cast_rows), :] = wv.astype(BF16)
        return carry
    lax.fori_loop(0, k_dim // cast_rows, body, 0)


def _mm_kernel(*refs, n_w, n_out, epilogue, cast_w, cast_rows, shift, out_scale):
    x_ref = refs[0]
    pos = 1
    w_refs = refs[pos:pos + n_w]
    pos += n_w
    wn_refs = refs[pos:pos + n_w] if shift else ()
    pos += n_w if shift else 0
    o_refs = refs[pos:pos + n_out]
    wb_refs = refs[pos + n_out:]
    if cast_w:
        @pl.when(pl.program_id(1) == 0)
        def _():
            _cast_weights(w_refs, wn_refs, wb_refs, cast_rows, shift)
        ws = wb_refs
    else:
        ws = w_refs
    x = x_ref[...]
    accs = [jnp.dot(x, w[...], preferred_element_type=F32) for w in ws]
    if epilogue == "none":
        r = accs[0]
    elif epilogue == "sigmoid":
        r = _sigmoid(accs[0])
    elif epilogue == "glu":
        r = accs[0] * _sigmoid(accs[1])
    else:
        r = accs[0] * _sigmoid(accs[0]) * accs[1]
    if out_scale != 1.0:
        r = r * out_scale
    for o_ref in o_refs:
        o_ref[...] = r.astype(o_ref.dtype)


def _mm(x, w, l, col_off, n_cols, *, out_dtype, epilogue="none", pair_off=None, tn=512, tm=None, out_scale=1.0,
        bf16_copy=False):
    M, K = x.shape
    n_w = 1 if pair_off is None else 2
    cast_w = w.dtype != BF16
    shift = col_off % LANE
    assert cast_w or not shift
    if tm is None:
        tm = _pick(M, (512,)) if K > 4096 else _pick(M, (1024, 512, 256))
    tm = min(tm, M)
    base = col_off - shift
    assert n_cols % tn == 0 and base % tn == 0 and (pair_off or 0) % tn == 0
    offs = [base // tn] if n_w == 1 else [base // tn, (base + pair_off) // tn]
    in_specs = [pl.BlockSpec((tm, K), lambda j, i: (i, 0))]
    for o in offs:
        in_specs.append(pl.BlockSpec((None, K, tn), lambda j, i, o=o: (l, 0, o + j)))
    if shift:
        for o in offs:
            in_specs.append(pl.BlockSpec((None, K, LANE), lambda j, i, o=o: (l, 0, (o + j + 1) * (tn // LANE))))
    scratch = [pltpu.VMEM((K, tn), BF16) for _ in range(n_w)] if cast_w else []
    dtypes = [out_dtype] + ([BF16] if bf16_copy else [])
    outs = pl.pallas_call(
        functools.partial(_mm_kernel, n_w=n_w, n_out=len(dtypes), epilogue=epilogue, cast_w=cast_w,
                          cast_rows=_pick(K, (256, 128, 64, 16)), shift=shift, out_scale=out_scale),
        grid=(n_cols // tn, M // tm),
        in_specs=in_specs,
        out_specs=[pl.BlockSpec((tm, tn), lambda j, i: (i, j)) for _ in dtypes],
        out_shape=[jax.ShapeDtypeStruct((M, n_cols), d) for d in dtypes],
        scratch_shapes=scratch,
        compiler_params=_cp("arbitrary", "arbitrary"),
    )(x, *([w] * (n_w * (2 if shift else 1))))
    return outs if bf16_copy else outs[0]


def _cast_kernel(x_ref, o_ref):
    o_ref[...] = x_ref[...].astype(o_ref.dtype)


def _cast_bf16(w):
    dep, K, N = w.shape
    tk = _pick(K, (512, 256, 128, 64, 16))
    return pl.pallas_call(
        _cast_kernel,
        grid=(dep, K // tk),
        in_specs=[pl.BlockSpec((None, tk, N), lambda d, k: (d, k, 0))],
        out_specs=pl.BlockSpec((None, tk, N), lambda d, k: (d, k, 0)),
        out_shape=jax.ShapeDtypeStruct(w.shape, BF16),
        compiler_params=_cp("arbitrary", "arbitrary"),
    )(w)


def _merge_kernel(a1, a2, a3, w1, w2, w3, g1, g2, g3, o_ref, wb1, wb2, wb3, *, cast_rows):
    @pl.when(pl.program_id(1) == 0)
    def _():
        _cast_weights((w1, w2, w3), (), (wb1, wb2, wb3), cast_rows, 0)

    acc = g1[...].astype(F32) * jnp.dot(a1[...], wb1[...], preferred_element_type=F32)
    acc = acc + g2[...].astype(F32) * jnp.dot(a2[...], wb2[...], preferred_element_type=F32)
    acc = acc + g3[...].astype(F32) * jnp.dot(a3[...], wb3[...], preferred_element_type=F32)
    o_ref[...] = acc.astype(o_ref.dtype)


def _merge_mm(a_conv, a_diff, a_nsa, w_conv_out, w_diff_out, w_nsa_out, gates, l):
    M, K = a_conv.shape
    D = w_conv_out.shape[2]
    tm = _pick(M, (512, 256))
    tn = 512
    nb = D // tn
    a_spec = pl.BlockSpec((tm, K), lambda j, i: (i, 0))
    w_spec = pl.BlockSpec((None, K, tn), lambda j, i: (l, 0, j))
    g_specs = [pl.BlockSpec((tm, tn), lambda j, i, s=s: (i, s * nb + j)) for s in range(3)]
    return pl.pallas_call(
        functools.partial(_merge_kernel, cast_rows=_pick(K, (256, 128))),
        grid=(nb, M // tm),
        in_specs=[a_spec] * 3 + [w_spec] * 3 + g_specs,
        out_specs=pl.BlockSpec((tm, tn), lambda j, i: (i, j)),
        out_shape=jax.ShapeDtypeStruct((M, D), BF16),
        scratch_shapes=[pltpu.VMEM((K, tn), BF16) for _ in range(3)],
        compiler_params=_cp("arbitrary", "arbitrary"),
    )(a_conv, a_diff, a_nsa, w_conv_out, w_diff_out, w_nsa_out, gates, gates, gates)


def _ada_kernel(c_ref, w_ref, b_ref, o_ref):
    c = c_ref[...]
    a = (c * _sigmoid(c)).astype(BF16)
    o_ref[...] = jnp.dot(a, w_ref[...].astype(BF16), preferred_element_type=F32) + b_ref[...]


def _ada(c_all, w_ada, b_ada):
    dep, D, N = w_ada.shape
    R = c_all.shape[0]
    tn = 512
    return pl.pallas_call(
        _ada_kernel,
        grid=(dep, N // tn),
        in_specs=[pl.BlockSpec((R, D), lambda d, j: (0, 0)),
                  pl.BlockSpec((None, D, tn), lambda d, j: (d, 0, j)),
                  pl.BlockSpec((None, 1, tn), lambda d, j: (d, 0, j))],
        out_specs=pl.BlockSpec((None, R, tn), lambda d, j: (d, 0, j)),
        out_shape=jax.ShapeDtypeStruct((dep, R, N), F32),
        compiler_params=_cp("arbitrary", "arbitrary"),
    )(c_all, w_ada, b_ada.reshape(dep, 1, N))


def _norm_kernel(*refs, has_resid, has_h):
    it = iter(refs)
    x = next(it)[...]
    if has_resid:
        y = next(it)[...]
        g_post = next(it)[...]
        gate = next(it)[...]
    if has_h:
        g_pre = next(it)[...]
        sc = next(it)[...]
        sh = next(it)[...]
    if has_resid:
        yn = y * lax.rsqrt(jnp.mean(y * y, axis=-1, keepdims=True) + 1e-6) * g_post
        x = x + gate * yn
        next(it)[...] = x
    if has_h:
        hn = x * lax.rsqrt(jnp.mean(x * x, axis=-1, keepdims=True) + 1e-6) * g_pre
        next(it)[...] = (hn * (1.0 + sc) + sh).astype(BF16)


def _norm(x, mod, *, y=None, g_post=None, l_post=None, gate_chunk=None,
          g_pre=None, l_pre=None, sc_chunk=None, sh_chunk=None):
    S, T, D = x.shape
    R = mod.shape[2]
    tr = _pick(T, (256, 128, 64, 32, 16, 8))
    mr = 1 if R == 1 else tr
    has_resid = y is not None
    has_h = g_pre is not None
    row = pl.BlockSpec((None, tr, D), lambda s, i: (s, i, 0))

    def gspec(l):
        return pl.BlockSpec((None, 1, D), lambda s, i: (l, 0, 0))

    def mspec(l, k):
        if R == 1:
            return pl.BlockSpec((None, None, 1, D), lambda s, i: (l, s, 0, k))
        return pl.BlockSpec((None, None, mr, D), lambda s, i: (l, s, i, k))

    args, specs = [x], [row]
    if has_resid:
        args += [y, g_post.reshape(-1, 1, D), mod]
        specs += [row, gspec(l_post), mspec(l_post, gate_chunk)]
    if has_h:
        args += [g_pre.reshape(-1, 1, D), mod, mod]
        specs += [gspec(l_pre), mspec(l_pre, sc_chunk), mspec(l_pre, sh_chunk)]
    out_shape, out_specs = [], []
    if has_resid:
        out_shape.append(jax.ShapeDtypeStruct((S, T, D), F32))
        out_specs.append(row)
    if has_h:
        out_shape.append(jax.ShapeDtypeStruct((S, T, D), BF16))
        out_specs.append(row)
    outs = pl.pallas_call(
        functools.partial(_norm_kernel, has_resid=has_resid, has_h=has_h),
        grid=(S, T // tr),
        in_specs=specs, out_specs=out_specs, out_shape=out_shape,
        compiler_params=_cp("arbitrary", "arbitrary"),
    )(*args)
    return outs


HIST_ROWS = 32


def _conv_kernel(hist_ref, cur_ref, w_ref, b_ref, g_ref, be_ref, o_ref, xin_ref, sh_ref, y_ref, *,
                 tt, width, zero_first_hist):
    C = cur_ref.shape[-1]
    hist = hist_ref[...]
    if zero_first_hist:
        hist = jnp.where(pl.program_id(1) > 0, hist, 0.0)
    xin_ref[0:HIST_ROWS, :] = hist
    xin_ref[HIST_ROWS:HIST_ROWS + tt, :] = cur_ref[...]
    cc = _pick(C, (256, 128))
    rc = min(64, tt)
    off = HIST_ROWS - (width - 1)
    n_sh = HIST_ROWS + tt - SUBLANE
    for res in range(1, SUBLANE):
        sh_ref[res - 1, 0:n_sh, :] = xin_ref[res:res + n_sh, :]

    def cbody(c, carry):
        c0 = pl.multiple_of(c * cc, cc)
        for r0 in range(0, tt, rc):
            acc = jnp.zeros((rc, cc), F32)
            for p in range(off, off + width):
                res = p % SUBLANE
                src = xin_ref if res == 0 else sh_ref.at[res - 1]
                acc = acc + src[pl.ds(r0 + p - res, rc), pl.ds(c0, cc)] * w_ref[pl.ds(p - off, 1), pl.ds(c0, cc)]
            y_ref[pl.ds(r0, rc), pl.ds(c0, cc)] = acc
        return carry
    lax.fori_loop(0, C // cc, cbody, 0)

    rr = min(32, tt)

    def rbody(r, carry):
        r0 = pl.multiple_of(r * rr, rr)
        y = y_ref[pl.ds(r0, rr), :] + b_ref[...]
        mu = jnp.mean(y, axis=-1, keepdims=True)
        yc = y - mu
        var = jnp.mean(yc * yc, axis=-1, keepdims=True)
        yn = yc * lax.rsqrt(var + 1e-5) * g_ref[...] + be_ref[...]
        o_ref[pl.ds(r0, rr), :] = (yn * _sigmoid(yn)).astype(o_ref.dtype)
        return carry
    lax.fori_loop(0, tt // rr, rbody, 0)


def _conv_module(u, hist, conv_w, conv_b, ln_g, ln_b, l):
    S, T, C = u.shape
    width = conv_w.shape[1]
    tt = _pick(T, (256, 128, 64, 32, 16))
    if hist is None:
        hist_arr = u
        hist_spec = pl.BlockSpec((None, HIST_ROWS, C),
                                 lambda s, i: (s, jnp.maximum(i * (tt // HIST_ROWS) - 1, 0), 0))
    else:
        hist_arr = hist
        hist_spec = pl.BlockSpec((None, HIST_ROWS, C), lambda s, i: (s, 0, 0))
    vec = pl.BlockSpec((None, 1, C), lambda s, i: (l, 0, 0))
    return pl.pallas_call(
        functools.partial(_conv_kernel, tt=tt, width=width, zero_first_hist=hist is None),
        grid=(S, T // tt),
        in_specs=[hist_spec,
                  pl.BlockSpec((None, tt, C), lambda s, i: (s, i, 0)),
                  pl.BlockSpec((None, width, C), lambda s, i: (l, 0, 0)),
                  vec, vec, vec],
        out_specs=pl.BlockSpec((None, tt, C), lambda s, i: (s, i, 0)),
        out_shape=jax.ShapeDtypeStruct((S, T, C), BF16),
        scratch_shapes=[pltpu.VMEM((HIST_ROWS + tt, C), F32), pltpu.VMEM((SUBLANE - 1, HIST_ROWS + tt, C), F32),
                        pltpu.VMEM((tt, C), F32)],
        compiler_params=_cp("arbitrary", "arbitrary"),
    )(hist_arr, u, conv_w, conv_b.reshape(-1, 1, C), ln_g.reshape(-1, 1, C), ln_b.reshape(-1, 1, C))


def _conv_state_kernel(*refs, depth, keep, n_new):
    srcs = refs[:depth]
    if n_new is None:
        o_ref = refs[depth]
        for l in range(depth):
            o_ref[l] = srcs[l][HIST_ROWS - keep:, :]
    else:
        st_ref, o_ref = refs[depth:]
        for l in range(depth):
            o_ref[l, 0:keep - n_new, :] = st_ref[l, n_new:keep, :]
            o_ref[l, keep - n_new:keep, :] = srcs[l][...]


def _conv_states(us, keep, state=None):
    depth = len(us)
    S, T, C = us[0].shape
    o_spec = pl.BlockSpec((depth, None, keep, C), lambda s: (0, s, 0, 0))
    if state is None:
        assert T % HIST_ROWS == 0 and keep <= HIST_ROWS
        in_specs = [pl.BlockSpec((None, HIST_ROWS, C), lambda s: (s, T // HIST_ROWS - 1, 0))] * depth
        args, n_new = list(us), None
    else:
        assert T < keep
        in_specs = [pl.BlockSpec((None, T, C), lambda s: (s, 0, 0))] * depth + [o_spec]
        args, n_new = list(us) + [state], T
    return pl.pallas_call(
        functools.partial(_conv_state_kernel, depth=depth, keep=keep, n_new=n_new),
        grid=(S,), in_specs=in_specs, out_specs=o_spec,
        out_shape=jax.ShapeDtypeStruct((depth, S, keep, C), F32),
        compiler_params=_cp("arbitrary"),
    )(*args)


def _lam(lq1, lk1, lq2, lk2, lam_init):
    a = jnp.exp(jnp.sum(lq1[...] * lk1[...], axis=-1, keepdims=True))
    b = jnp.exp(jnp.sum(lq2[...] * lk2[...], axis=-1, keepdims=True))
    return a - b + lam_init


def _subln(o, g, lam_init):
    on = o * lax.rsqrt(jnp.mean(o * o, axis=-1, keepdims=True) + 1e-6) * g
    return on * (1.0 - lam_init)


def _diff_prompt_kernel(q_ref, k_ref, v_ref, lq1, lk1, lq2, lk2, gs_ref, o_ref, acc_ref, *, tq, tk, hp, lam_init):
    qi = pl.program_id(2)
    hw = 2 * HEAD_DIM
    q = q_ref[...]
    acc_ref[...] = jnp.zeros_like(acc_ref)
    row = qi * tq + _iota((tq, tk), 0)
    col0 = _iota((tq, tk), 1)

    def make_body(masked):
        def body(kc, carry):
            k0 = pl.multiple_of(kc * tk, tk)
            kblk = k_ref[pl.ds(k0, tk), :]
            vblk = v_ref[pl.ds(k0, tk), :]
            if masked:
                mask = (k0 + col0) <= row
            new = []
            for idx in range(2 * hp):
                lo = idx * HEAD_DIM
                m_old, l_old = carry[2 * idx], carry[2 * idx + 1]
                s = lax.dot_general(q[:, lo:lo + HEAD_DIM], kblk[:, lo:lo + HEAD_DIM], NT_DIMS,
                                    preferred_element_type=F32)
                if masked:
                    s = jnp.where(mask, s, NEG)
                m_new = jnp.maximum(m_old, jnp.max(s, axis=-1, keepdims=True))
                p = jnp.exp(s - m_new)
                alpha = jnp.exp(m_old - m_new)
                vh = vblk[:, (idx // 2) * hw:(idx // 2 + 1) * hw]
                acc_ref[idx] = alpha * acc_ref[idx] + jnp.dot(p.astype(BF16), vh, preferred_element_type=F32)
                new += [m_new, alpha * l_old + jnp.sum(p, axis=-1, keepdims=True)]
            return tuple(new)
        return body

    init = (jnp.full((tq, 1), NEG, F32), jnp.zeros((tq, 1), F32)) * (2 * hp)
    n_full = (qi * tq) // tk
    carry = lax.fori_loop(0, n_full, make_body(False), init)
    carry = make_body(True)(n_full, carry)
    lam = _lam(lq1, lk1, lq2, lk2, lam_init)
    for hh in range(hp):
        o = acc_ref[2 * hh] / carry[4 * hh + 1] - lam * (acc_ref[2 * hh + 1] / carry[4 * hh + 3])
        o_ref[:, hh * hw:(hh + 1) * hw] = _subln(o, gs_ref[...], lam_init).astype(o_ref.dtype)


def _diff_prompt(dq, dk, dv, lam_params, g_subln, l, S, T, lam_init):
    M, DW = dq.shape
    hw = 2 * HEAD_DIM
    H = DW // hw
    hp = 2 if H % 2 == 0 else 1
    tq = _pick(T, (256, 128))
    tk = 2 * tq
    assert T % tk == 0
    nq = T // tq
    vec = pl.BlockSpec((None, 1, HEAD_DIM), lambda s, h, i: (l, 0, 0))
    return pl.pallas_call(
        functools.partial(_diff_prompt_kernel, tq=tq, tk=tk, hp=hp, lam_init=lam_init),
        grid=(S, H // hp, nq),
        in_specs=[pl.BlockSpec((tq, hp * hw), lambda s, h, i: (s * nq + i, h)),
                  pl.BlockSpec((T, hp * hw), lambda s, h, i: (s, h)),
                  pl.BlockSpec((T, hp * hw), lambda s, h, i: (s, h)),
                  vec, vec, vec, vec,
                  pl.BlockSpec((None, 1, hw), lambda s, h, i: (l, 0, 0))],
        out_specs=pl.BlockSpec((tq, hp * hw), lambda s, h, i: (s * nq + i, h)),
        out_shape=jax.ShapeDtypeStruct((M, DW), BF16),
        scratch_shapes=[pltpu.VMEM((2 * hp, tq, hw), F32)],
        compiler_params=_cp("arbitrary", "arbitrary", "arbitrary"),
    )(dq, dk, dv, *[p.reshape(-1, 1, HEAD_DIM) for p in lam_params], g_subln.reshape(-1, 1, hw))


def _diff_sample_kernel(pt_ref, q_ref, kc_ref, vc_ref, kn_ref, vn_ref, lq1, lk1, lq2, lk2, gs_ref, o_ref,
                        m_ref, l_ref, acc_ref, *, n_pages, n_new, n_heads, lam_init):
    p = pl.program_id(1)
    R, hw = q_ref.shape

    def flat(ref):
        return ref[...].reshape(ref.shape[0] * n_heads, hw).astype(BF16)

    def own_head(nk):
        return _vmod(_iota((R, nk), 1), n_heads) == _vmod(_vdiv(_iota((R, nk), 0), n_new), n_heads)

    @pl.when(p == 0)
    def _():
        m_ref[...] = jnp.full_like(m_ref, NEG)
        l_ref[...] = jnp.zeros_like(l_ref)
        acc_ref[...] = jnp.zeros_like(acc_ref)

    def step(k_ref, v_ref, valid):
        s = lax.dot_general(q_ref[...], flat(k_ref), NT_DIMS, preferred_element_type=F32)
        s = jnp.where(valid, s, MASKED)
        m_old = m_ref[...]
        m_new = jnp.maximum(m_old, jnp.max(s, axis=-1, keepdims=True))
        pr = jnp.exp(s - m_new)
        alpha = jnp.exp(m_old - m_new)
        l_ref[...] = alpha * l_ref[...] + jnp.sum(pr, axis=-1, keepdims=True)
        acc_ref[...] = alpha * acc_ref[...] + jnp.dot(pr.astype(BF16), flat(v_ref), preferred_element_type=F32)
        m_ref[...] = m_new

    @pl.when(p < n_pages)
    def _():
        step(kc_ref, vc_ref, own_head(kc_ref.shape[0] * n_heads))

    @pl.when(p == n_pages)
    def _():
        nk = kn_ref.shape[0] * n_heads
        j = _vdiv(_iota((R, nk), 1), n_heads)
        t = _vmod(_iota((R, nk), 0), n_new)
        step(kn_ref, vn_ref, own_head(nk) & (j < n_new) & (j <= t))
        lam = _lam(lq1, lk1, lq2, lk2, lam_init)
        on = acc_ref[...] / l_ref[...]
        o = on[:R // 2] - lam * on[R // 2:]
        o_ref[...] = _subln(o, gs_ref[...], lam_init)


def _diff_sample(qbd, cache_k, cache_v, k_new, v_new, page_table, lam_params, g_subln, l, n_new, lam_init):
    S, R, hw = qbd.shape
    H = cache_k.shape[3]
    assert R == 2 * H * n_new and n_new & (n_new - 1) == 0
    n_pages = page_table.shape[1]
    page = cache_k.shape[2]
    nr = k_new.shape[1]
    vec = pl.BlockSpec((None, 1, HEAD_DIM), lambda s, p, pt: (l, 0, 0))
    cache_spec = pl.BlockSpec((None, None, page, H, hw),
                              lambda s, p, pt: (l, pt[s, jnp.minimum(p, n_pages - 1)], 0, 0, 0))
    new_spec = pl.BlockSpec((None, nr, H, hw), lambda s, p, pt: (s, 0, 0, 0))
    grid_spec = pltpu.PrefetchScalarGridSpec(
        num_scalar_prefetch=1,
        grid=(S, n_pages + 1),
        in_specs=[pl.BlockSpec((None, R, hw), lambda s, p, pt: (s, 0, 0)),
                  cache_spec, cache_spec, new_spec, new_spec,
                  vec, vec, vec, vec,
                  pl.BlockSpec((None, 1, hw), lambda s, p, pt: (l, 0, 0))],
        out_specs=pl.BlockSpec((None, R // 2, hw), lambda s, p, pt: (s, 0, 0)),
        scratch_shapes=[pltpu.VMEM((R, 1), F32), pltpu.VMEM((R, 1), F32), pltpu.VMEM((R, hw), F32)],
    )
    return pl.pallas_call(
        functools.partial(_diff_sample_kernel, n_pages=n_pages, n_new=n_new, n_heads=H, lam_init=lam_init),
        grid_spec=grid_spec,
        out_shape=jax.ShapeDtypeStruct((S, R // 2, hw), F32),
        compiler_params=_cp("arbitrary", "arbitrary"),
    )(page_table, qbd, cache_k, cache_v, k_new, v_new,
      *[p.reshape(-1, 1, HEAD_DIM) for p in lam_params], g_subln.reshape(-1, 1, hw))


def _compress_prompt_kernel(k_ref, v_ref, wk_ref, wv_ref, ok_ref, ov_ref):
    for src, w_ref, o_ref in ((k_ref, wk_ref, ok_ref), (v_ref, wv_ref, ov_ref)):
        w = w_ref[...]
        for n in range(src.shape[0] // NSA_BLOCK):
            o_ref[n:n + 1, :] = jnp.sum(src[n * NSA_BLOCK:(n + 1) * NSA_BLOCK, :] * w, axis=0, keepdims=True)


def _compress_prompt(kv, wc_k, wc_v, l):
    M = kv.shape[0]
    kvw = wc_k.shape[-1]
    rows = SUBLANE * NSA_BLOCK
    w_spec = pl.BlockSpec((None, NSA_BLOCK, kvw), lambda i: (l, 0, 0))
    o_spec = pl.BlockSpec((SUBLANE, kvw), lambda i: (i, 0))
    o_shape = jax.ShapeDtypeStruct((M // NSA_BLOCK, kvw), F32)
    return pl.pallas_call(
        _compress_prompt_kernel,
        grid=(M // rows,),
        in_specs=[pl.BlockSpec((rows, kvw), lambda i: (i, 0)), pl.BlockSpec((rows, kvw), lambda i: (i, 1)),
                  w_spec, w_spec],
        out_specs=[o_spec, o_spec], out_shape=[o_shape, o_shape],
        compiler_params=_cp("arbitrary"),
    )(kv, kv, wc_k, wc_v)


def _compress_sample_kernel(pt_ref, *refs, n_src):
    k_srcs = refs[:n_src]
    v_srcs = refs[n_src:2 * n_src]
    wk_ref, wv_ref, ok_ref, ov_ref = refs[2 * n_src:]
    rows = NSA_BLOCK * G_NSA
    for srcs, w_ref, o_ref in ((k_srcs, wk_ref, ok_ref), (v_srcs, wv_ref, ov_ref)):
        w = w_ref[...].reshape(rows, HEAD_DIM)
        n = 0
        for src in srcs:
            flat = src[...].reshape(src.shape[0] * G_NSA, HEAD_DIM)
            for b0 in range(0, flat.shape[0], rows):
                r8 = (flat[b0:b0 + rows] * w).reshape(rows // SUBLANE, SUBLANE, HEAD_DIM).sum(axis=0)
                r4 = r8[0:G_NSA]
                for k in range(1, SUBLANE // G_NSA):
                    r4 = r4 + r8[k * G_NSA:(k + 1) * G_NSA]
                for g in range(G_NSA):
                    o_ref[g, n:n + 1, :] = r4[g:g + 1, :]
                n += 1


def _compress_sample(cache_k, cache_v, page_table, wc_k, wc_v, l):
    S, n_pages = page_table.shape
    page = cache_k.shape[2]
    per_step = (SUBLANE * NSA_BLOCK) // page
    assert n_pages % per_step == 0
    nblk = n_pages * page // NSA_BLOCK

    def cspec(i):
        return pl.BlockSpec((None, None, page, G_NSA, HEAD_DIM),
                            lambda s, t, pt: (l, pt[s, t * per_step + i], 0, 0, 0))
    w_spec = pl.BlockSpec((None, NSA_BLOCK, G_NSA, HEAD_DIM), lambda s, t, pt: (l, 0, 0, 0))
    o_spec = pl.BlockSpec((None, G_NSA, SUBLANE, HEAD_DIM), lambda s, t, pt: (s, 0, t, 0))
    o_shape = jax.ShapeDtypeStruct((S, G_NSA, nblk, HEAD_DIM), F32)
    grid_spec = pltpu.PrefetchScalarGridSpec(
        num_scalar_prefetch=1, grid=(S, n_pages // per_step),
        in_specs=[cspec(i) for i in range(per_step)] * 2 + [w_spec, w_spec],
        out_specs=[o_spec, o_spec])
    return pl.pallas_call(
        functools.partial(_compress_sample_kernel, n_src=per_step),
        grid_spec=grid_spec, out_shape=[o_shape, o_shape],
        compiler_params=_cp("arbitrary", "arbitrary"),
    )(page_table, *([cache_k] * per_step), *([cache_v] * per_step), wc_k, wc_v)


def _rank_select(score, n_keep):
    R, N = score.shape
    n_i = _iota((R, N), 1)
    rank = jnp.zeros((R, N), F32)
    for m in range(N):
        col = score[:, m:m + 1]
        beats = jnp.where(col > score, 1.0, jnp.where(col == score, jnp.where(n_i > m, 1.0, 0.0), 0.0))
        rank = rank + beats
    return jnp.where(rank < n_keep, 1.0, 0.0)


def _rank_select_t(score, n_keep):
    N, R = score.shape
    n_i = _iota((N, R), 0)
    rank = jnp.zeros((N, R), F32)
    for m in range(N):
        row = score[m:m + 1, :]
        beats = jnp.where(row > score, 1.0, jnp.where(row == score, jnp.where(n_i > m, 1.0, 0.0), 0.0))
        rank = rank + beats
    return jnp.where(rank < n_keep, 1.0, 0.0)


def _softmax_rows(s):
    m = jnp.max(s, axis=-1, keepdims=True)
    e = jnp.exp(s - m)
    return e / jnp.sum(e, axis=-1, keepdims=True)


def _nsa_prompt_kernel(q_ref, kcb_ref, vcb_ref, sk_ref, sv_ref, wk_ref, wv_ref, g_ref, o_ref, acc_ref, *,
                       tq, tk, hg):
    qi = pl.program_id(2)
    nblk = kcb_ref.shape[0]
    R = hg * tq
    q = jnp.concatenate([q_ref[:, hh * HEAD_DIM:(hh + 1) * HEAD_DIM] for hh in range(hg)], axis=0)

    kcb = kcb_ref[...].astype(BF16)
    vcb = vcb_ref[...].astype(BF16)
    t_rn = qi * tq + _vmod(_iota((R, nblk), 0), tq)
    n_rn = _iota((R, nblk), 1)
    complete = (n_rn + 1) * NSA_BLOCK <= t_rn + 1
    s_c = lax.dot_general(q, kcb, NT_DIMS, preferred_element_type=F32)
    p_c = jnp.where(complete, _softmax_rows(jnp.where(complete, s_c, NEG)), 0.0)
    o_c = jnp.dot(p_c.astype(BF16), vcb, preferred_element_type=F32)

    t_nr = qi * tq + _vmod(_iota((nblk, R), 1), tq)
    complete_nr = (_iota((nblk, R), 0) + 1) * NSA_BLOCK <= t_nr + 1
    s_t = jnp.where(complete_nr, lax.dot_general(kcb, q, NT_DIMS, preferred_element_type=F32), NEG)
    e_t = jnp.exp(s_t - jnp.max(s_t, axis=0, keepdims=True))
    p_t = jnp.where(complete_nr, e_t / jnp.sum(e_t, axis=0, keepdims=True), 0.0)
    imp = p_t[:, 0:tq]
    for hh in range(1, hg):
        imp = imp + p_t[:, hh * tq:(hh + 1) * tq]
    t_nt = qi * tq + _iota((nblk, tq), 1)
    n_nt = _iota((nblk, tq), 0)
    score = jnp.where(n_nt == _vdiv(t_nt, NSA_BLOCK), FORCE,
                      jnp.where((n_nt + 1) * NSA_BLOCK <= t_nt + 1, imp, -1.0))
    sel = (_rank_select_t(score, min(NSA_TOPK, nblk)) * jnp.where(score >= 0.0, 1.0, 0.0)).T
    bias_r = jnp.concatenate([(sel - 1.0) * (-NEG)] * hg, axis=0).astype(BF16)

    acc_ref[...] = jnp.zeros_like(acc_ref)
    t_rk = qi * tq + _vmod(_iota((R, tk), 0), tq)
    j_rk = _iota((R, tk), 1)

    def make_body(masked):
        def body(kc, carry):
            m_old, l_old = carry
            k0 = pl.multiple_of(kc * tk, tk)
            blk = _vdiv(k0 + _iota((nblk, tk), 1), NSA_BLOCK)
            expand = jnp.where(_iota((nblk, tk), 0) == blk, 1.0, 0.0).astype(BF16)
            s = (lax.dot_general(q, sk_ref[pl.ds(k0, tk), :], NT_DIMS, preferred_element_type=F32)
                 + jnp.dot(bias_r, expand, preferred_element_type=F32))
            if masked:
                s = jnp.where(k0 + j_rk <= t_rk, s, NEG)
            m_new = jnp.maximum(m_old, jnp.max(s, axis=-1, keepdims=True))
            p = jnp.exp(s - m_new)
            alpha = jnp.exp(m_old - m_new)
            acc_ref[...] = alpha * acc_ref[...] + jnp.dot(p.astype(BF16), sv_ref[pl.ds(k0, tk), :],
                                                          preferred_element_type=F32)
            return m_new, alpha * l_old + jnp.sum(p, axis=-1, keepdims=True)
        return body

    n_full = (qi * tq) // tk
    carry = lax.fori_loop(0, n_full, make_body(False), (jnp.full((R, 1), NEG, F32), jnp.zeros((R, 1), F32)))
    _, l_s = make_body(True)(n_full, carry)
    o_s = acc_ref[...] / l_s

    wlen = WINDOW + tq
    w0 = pl.multiple_of(jnp.maximum(qi * tq - WINDOW, 0), tq)
    kpos = w0 + _iota((R, wlen), 1)
    t_rw = qi * tq + _vmod(_iota((R, wlen), 0), tq)
    valid = (kpos <= t_rw) & (kpos > t_rw - WINDOW)
    s_w = lax.dot_general(q, wk_ref[pl.ds(w0, wlen), :], NT_DIMS, preferred_element_type=F32)
    s_w = jnp.where(valid, s_w, NEG)
    p_w = jnp.exp(s_w - jnp.max(s_w, axis=-1, keepdims=True))
    o_w = (jnp.dot(p_w.astype(BF16), wv_ref[pl.ds(w0, wlen), :], preferred_element_type=F32)
           / jnp.sum(p_w, axis=-1, keepdims=True))

    g = g_ref[...]
    for hh in range(hg):
        rs = slice(hh * tq, (hh + 1) * tq)
        o = (g[:, 3 * hh:3 * hh + 1] * o_c[rs] + g[:, 3 * hh + 1:3 * hh + 2] * o_s[rs]
             + g[:, 3 * hh + 2:3 * hh + 3] * o_w[rs])
        o_ref[:, hh * HEAD_DIM:(hh + 1) * HEAD_DIM] = o.astype(o_ref.dtype)


def _nsa_prompt(nq, kcb, vcb, kvb, gates, S, T, hg):
    M, QW = nq.shape
    tq = _pick(T, (128, 64))
    tk = 4 * tq
    assert T % tk == 0 and T >= WINDOW + tq
    nqt = T // tq
    nblk = T // NSA_BLOCK
    gw = hg * HEAD_DIM

    def kv_spec(seg):
        return pl.BlockSpec((T, HEAD_DIM), lambda s, g, i: (s, seg * G_NSA + g))
    cb_spec = pl.BlockSpec((nblk, HEAD_DIM), lambda s, g, i: (s, g))
    return pl.pallas_call(
        functools.partial(_nsa_prompt_kernel, tq=tq, tk=tk, hg=hg),
        grid=(S, G_NSA, nqt),
        in_specs=[pl.BlockSpec((tq, gw), lambda s, g, i: (s * nqt + i, g)),
                  cb_spec, cb_spec, kv_spec(2), kv_spec(3), kv_spec(4), kv_spec(5),
                  pl.BlockSpec((tq, LANE), lambda s, g, i: (s * nqt + i, g))],
        out_specs=pl.BlockSpec((tq, gw), lambda s, g, i: (s * nqt + i, g)),
        out_shape=jax.ShapeDtypeStruct((M, QW), BF16),
        scratch_shapes=[pltpu.VMEM((hg * tq, HEAD_DIM), F32)],
        compiler_params=_cp("arbitrary", "arbitrary", "arbitrary"),
    )(nq, kcb, vcb, kvb, kvb, kvb, kvb, gates)


def _nsa_sample_kernel(pt_ref, q_ref, kcb_ref, vcb_ref, sk_ref, sv_ref, skn_ref, svn_ref, wkn_ref, wvn_ref,
                       wk_ref, wv_ref, g_ref, o_ref, bias_ref, oc_ref, m_ref, l_ref, acc_ref, *,
                       n_pages, n_new, hg):
    p = pl.program_id(1)
    rpg = hg * SUBLANE
    R = G_NSA * rpg
    page = sk_ref.shape[0]
    nblk_past = kcb_ref.shape[1]
    bpp = page // NSA_BLOCK

    def per_group(fn):
        return jnp.concatenate([fn(g) for g in range(G_NSA)], axis=0)

    def flat(ref):
        return ref[...].reshape(ref.shape[0] * G_NSA, HEAD_DIM).astype(BF16)

    def key_of(nk):
        return _vdiv(_iota((R, nk), 1), G_NSA)

    def t_of(nk):
        return _vmod(_iota((R, nk), 0), SUBLANE)

    def qk(k_ref):
        nk = k_ref.shape[0] * G_NSA
        s = lax.dot_general(q_ref[...], flat(k_ref), NT_DIMS, preferred_element_type=F32)
        own = _vmod(_iota((R, nk), 1), G_NSA) == _vdiv(_iota((R, nk), 0), rpg)
        return jnp.where(own, s, MASKED)

    def pv(pr, v_ref):
        return jnp.dot(pr.astype(BF16), flat(v_ref), preferred_element_type=F32)

    @pl.when(p == 0)
    def _():
        p_c = _softmax_rows(per_group(lambda g: lax.dot_general(
            q_ref[g * rpg:(g + 1) * rpg, :], kcb_ref[g].astype(BF16), NT_DIMS,
            preferred_element_type=F32)))
        pcb = p_c.astype(BF16)
        oc_ref[...] = per_group(lambda g: jnp.dot(pcb[g * rpg:(g + 1) * rpg], vcb_ref[g].astype(BF16),
                                                  preferred_element_type=F32))

        def group_bias(g):
            imp = p_c[g * rpg:g * rpg + SUBLANE]
            for hh in range(1, hg):
                imp = imp + p_c[g * rpg + hh * SUBLANE:g * rpg + (hh + 1) * SUBLANE]
            sel = _rank_select(imp, min(NSA_TOPK, nblk_past + 1) - 1)
            return jnp.concatenate([(sel - 1.0) * (-NEG)] * hg, axis=0)
        bias_ref[...] = per_group(group_bias).astype(BF16)
        m_ref[...] = jnp.full_like(m_ref, NEG)
        l_ref[...] = jnp.zeros_like(l_ref)
        acc_ref[...] = jnp.zeros_like(acc_ref)

    def step(s, v_ref):
        m_old = m_ref[...]
        m_new = jnp.maximum(m_old, jnp.max(s, axis=-1, keepdims=True))
        pr = jnp.exp(s - m_new)
        alpha = jnp.exp(m_old - m_new)
        l_ref[...] = alpha * l_ref[...] + jnp.sum(pr, axis=-1, keepdims=True)
        acc_ref[...] = alpha * acc_ref[...] + pv(pr, v_ref)
        m_ref[...] = m_new

    nk = page * G_NSA
    blk = p * bpp + _vdiv(_iota((nblk_past, nk), 1), NSA_BLOCK * G_NSA)
    expand = jnp.where(_iota((nblk_past, nk), 0) == blk, 1.0, 0.0).astype(BF16)
    step(qk(sk_ref) + jnp.dot(bias_ref[...], expand, preferred_element_type=F32), sv_ref)

    @pl.when(p == n_pages - 1)
    def _():
        nn = skn_ref.shape[0] * G_NSA
        new_valid = (key_of(nn) < n_new) & (key_of(nn) <= t_of(nn))
        step(jnp.where(new_valid, qk(skn_ref), MASKED), svn_ref)
        o_s = acc_ref[...] / l_ref[...]
        wb = wk_ref.shape[0]
        buf_valid = key_of(wb * G_NSA) > t_of(wb * G_NSA) + (wb - WINDOW)
        s_b = jnp.where(buf_valid, qk(wk_ref), MASKED)
        s_n = jnp.where(new_valid, qk(wkn_ref), MASKED)
        m = jnp.maximum(jnp.max(s_b, axis=-1, keepdims=True), jnp.max(s_n, axis=-1, keepdims=True))
        p_b = jnp.exp(s_b - m)
        p_n = jnp.exp(s_n - m)
        lw = jnp.sum(p_b, axis=-1, keepdims=True) + jnp.sum(p_n, axis=-1, keepdims=True)
        o_w = (pv(p_b, wv_ref) + pv(p_n, wvn_ref)) / lw
        g = g_ref[...]
        o_ref[...] = g[:, 0:1] * oc_ref[...] + g[:, 1:2] * o_s + g[:, 2:3] * o_w


def _nsa_sample(q, kcb, vcb, cache_sk, cache_sv, sk_new, sv_new, wk_new, wv_new, win_k, win_v, gates_r,
                page_table, l, n_new, hg):
    S, R, _ = q.shape
    assert R == G_NSA * hg * SUBLANE
    n_pages = page_table.shape[1]
    page = cache_sk.shape[2]
    nblk_past = kcb.shape[2]
    wb = win_k.shape[2]
    nr = sk_new.shape[1]

    def per_s(*shape):
        return pl.BlockSpec((None,) + shape, lambda s, p, pt: (s,) + (0,) * len(shape))
    cache_spec = pl.BlockSpec((None, None, page, G_NSA, HEAD_DIM), lambda s, p, pt: (l, pt[s, p], 0, 0, 0))
    win_spec = pl.BlockSpec((None, None, wb, G_NSA, HEAD_DIM), lambda s, p, pt: (l, s, 0, 0, 0))
    new_spec = per_s(nr, G_NSA, HEAD_DIM)
    cb_spec = per_s(G_NSA, nblk_past, HEAD_DIM)
    grid_spec = pltpu.PrefetchScalarGridSpec(
        num_scalar_prefetch=1, grid=(S, n_pages),
        in_specs=[per_s(R, HEAD_DIM), cb_spec, cb_spec, cache_spec, cache_spec,
                  new_spec, new_spec, new_spec, new_spec, win_spec, win_spec, per_s(R, LANE)],
        out_specs=per_s(R, HEAD_DIM),
        scratch_shapes=[pltpu.VMEM((R, nblk_past), BF16), pltpu.VMEM((R, HEAD_DIM), F32),
                        pltpu.VMEM((R, 1), F32), pltpu.VMEM((R, 1), F32), pltpu.VMEM((R, HEAD_DIM), F32)])
    return pl.pallas_call(
        functools.partial(_nsa_sample_kernel, n_pages=n_pages, n_new=n_new, hg=hg),
        grid_spec=grid_spec,
        out_shape=jax.ShapeDtypeStruct((S, R, HEAD_DIM), F32),
        compiler_params=_cp("arbitrary", "arbitrary"),
    )(page_table, q, kcb, vcb, cache_sk, cache_sv, sk_new, sv_new, wk_new, wv_new, win_k, win_v, gates_r)


def kernel(x_prompt, x_sample, cache_diff_k, cache_diff_v, cache_cmp_k, cache_cmp_v, cache_sel_k, cache_sel_v, state_win_k, state_win_v, state_conv, page_table, c_prompt, c_sample, w_ada, b_ada, g_pre_mix, g_post_mix, g_pre_ffn, g_post_ffn, w_in, conv_w, conv_b, conv_ln_g, conv_ln_b, w_conv_out, lam_q1, lam_k1, lam_q2, lam_k2, g_subln, w_diff_out, w_cmp_k, w_cmp_v, w_nsa_out, w_out, w_gate_up, w_down):
    B, T, D = x_prompt.shape
    Bs, Ts, _ = x_sample.shape
    depth = w_in.shape[0]
    C = conv_w.shape[2]
    DW = w_diff_out.shape[1]
    H = DW // (2 * HEAD_DIM)
    QW = w_nsa_out.shape[1]
    hg = QW // (HEAD_DIM * G_NSA)
    KVW = G_NSA * HEAD_DIM
    n_gate = 3 * G_NSA * hg
    d_ff = w_down.shape[1]
    cw = conv_w.shape[1]
    off_dq = 2 * C
    off_dk = off_dq + DW
    off_dv = off_dk + DW
    off_nq = off_dv + DW
    off_kv = off_nq + QW
    off_ng = off_kv + 6 * KVW
    off_mg = off_ng + n_gate
    assert off_mg + 3 * D == w_in.shape[2] and Ts <= SUBLANE and off_ng % LANE == 0 and n_gate <= LANE

    w_down_b = _cast_bf16(w_down)
    wc_k2 = w_cmp_k.reshape(depth, NSA_BLOCK, KVW)
    wc_v2 = w_cmp_v.reshape(depth, NSA_BLOCK, KVW)

    n_seq = B + Bs
    c_all = jnp.pad(jnp.concatenate([c_prompt, c_sample], axis=0), ((0, -n_seq % SUBLANE), (0, 0)))
    mod = _ada(c_all, w_ada, b_ada)
    mod_p = mod[:, :B].reshape(depth, B, 1, 6 * D)
    mod_s = jnp.repeat(mod[:, B:n_seq], Ts, axis=1).reshape(depth, 1, Bs * Ts, 6 * D)

    Mp, Ms = B * T, Bs * Ts
    xp = x_prompt
    xs = x_sample.reshape(1, Ms, D)
    (hp,) = _norm(xp, mod_p, g_pre=g_pre_mix, l_pre=0, sc_chunk=1, sh_chunk=0)
    (hs,) = _norm(xs, mod_s, g_pre=g_pre_mix, l_pre=0, sc_chunk=1, sh_chunk=0)

    eye_2 = jnp.eye(2, dtype=BF16)
    outs = {n: [] for n in ("p_dk", "p_dv", "p_ck", "p_cv", "p_sk", "p_sv", "p_wk", "p_wv", "p_conv",
                            "s_dk", "s_dv", "s_ck", "s_cv", "s_sk", "s_sv", "s_wk", "s_wv", "s_conv")}

    for l in range(depth):
        lam_init = 0.8 - 0.6 * math.exp(-0.3 * l)
        lam_params = (lam_q1, lam_k1, lam_q2, lam_k2)

        def project(h2d):
            wide = dict(tm=512, tn=1024)
            u = _mm(h2d, w_in, l, 0, C, out_dtype=F32, epilogue="glu", pair_off=C, tm=512, tn=512)
            dq = _mm(h2d, w_in, l, off_dq, DW, out_dtype=BF16, out_scale=QK_SCALE, **wide)
            dk, dkb = _mm(h2d, w_in, l, off_dk, DW, out_dtype=F32, bf16_copy=True, **wide)
            dv, dvb = _mm(h2d, w_in, l, off_dv, DW, out_dtype=F32, bf16_copy=True, **wide)
            nq = _mm(h2d, w_in, l, off_nq, QW, out_dtype=BF16, out_scale=QK_SCALE, **wide)
            kv, kvb = _mm(h2d, w_in, l, off_kv, 6 * KVW, out_dtype=F32, bf16_copy=True, **wide)
            sg = _mm(h2d, w_in, l, off_ng, LANE, out_dtype=F32, epilogue="sigmoid", tn=LANE)
            sg = jnp.pad(sg[:, :n_gate].reshape(-1, G_NSA, 3 * hg),
                         ((0, 0), (0, 0), (0, LANE - 3 * hg))).reshape(-1, G_NSA * LANE)
            smg = _mm(h2d, w_in, l, off_mg, 3 * D, out_dtype=BF16, epilogue="sigmoid", **wide)
            return u, dq, dk, dkb, dv, dvb, nq, kv, kvb, sg, smg

        def finish(x3, mod_g, a_conv, a_diff, a_nsa, smg):
            merged = _merge_mm(a_conv, a_diff, a_nsa, w_conv_out, w_diff_out, w_nsa_out, smg, l)
            y = _mm(merged, w_out, l, 0, D, out_dtype=F32, tm=512, tn=1024)
            x1, h2 = _norm(x3, mod_g, y=y.reshape(x3.shape), g_post=g_post_mix, l_post=l, gate_chunk=2,
                           g_pre=g_pre_ffn, l_pre=l, sc_chunk=4, sh_chunk=3)
            act = _mm(h2.reshape(-1, D), w_gate_up, l, 0, d_ff, out_dtype=BF16, epilogue="swiglu",
                      pair_off=d_ff, tn=256)
            f = _mm(act, w_down_b, l, 0, D, out_dtype=F32)
            if l + 1 < depth:
                return _norm(x1, mod_g, y=f.reshape(x3.shape), g_post=g_post_ffn, l_post=l, gate_chunk=5,
                             g_pre=g_pre_mix, l_pre=l + 1, sc_chunk=1, sh_chunk=0)
            (x2,) = _norm(x1, mod_g, y=f.reshape(x3.shape), g_post=g_post_ffn, l_post=l, gate_chunk=5)
            return x2, None

        u, dq, dk, dkb, dv, dvb, nq, kv, kvb, sg, smg = project(hp.reshape(Mp, D))
        a_conv = _conv_module(u.reshape(B, T, C), None, conv_w, conv_b, conv_ln_g, conv_ln_b, l)
        a_diff = _diff_prompt(dq, dkb, dvb, lam_params, g_subln, l, B, T, lam_init)
        kcb, vcb = _compress_prompt(kv, wc_k2, wc_v2, l)
        a_nsa = _nsa_prompt(nq, kcb, vcb, kvb, sg, B, T, hg)
        xp, hp = finish(xp, mod_p, a_conv.reshape(Mp, C), a_diff, a_nsa, smg)
        kv5 = kv.reshape(B, T, 6, G_NSA, HEAD_DIM)
        keep = min(WINDOW, T)
        outs["p_dk"].append(dk.reshape(B, T, H, 2 * HEAD_DIM))
        outs["p_dv"].append(dv.reshape(B, T, H, 2 * HEAD_DIM))
        for i, n in enumerate(("p_ck", "p_cv", "p_sk", "p_sv")):
            outs[n].append(kv5[:, :, i])
        outs["p_wk"].append(kv5[:, T - keep:, 4])
        outs["p_wv"].append(kv5[:, T - keep:, 5])
        outs["p_conv"].append(u.reshape(B, T, C))

        u, dq, dk, _, dv, _, nq, kv, _, sg, smg = project(hs.reshape(Ms, D))
        u3 = u.reshape(Bs, Ts, C)
        hist = jnp.pad(state_conv[l], ((0, 0), (HIST_ROWS - (cw - 1), 0), (0, 0)))
        a_conv = _conv_module(jnp.pad(u3, ((0, 0), (0, 16 - Ts), (0, 0))), hist,
                              conv_w, conv_b, conv_ln_g, conv_ln_b, l)[:, :Ts].reshape(Ms, C)

        def new_rows(a, *tail):
            return jnp.pad(a.reshape((Bs, Ts) + tail), ((0, 0), (0, SUBLANE - Ts)) + ((0, 0),) * len(tail))

        q5 = dq.reshape(Bs, Ts, H, 2, HEAD_DIM)
        qbd = jnp.einsum("bthcd,cj->bchtjd", q5, eye_2).reshape(Bs, 2 * H * Ts, 2 * HEAD_DIM)
        o_d = _diff_sample(qbd, cache_diff_k, cache_diff_v, new_rows(dk, H, 2 * HEAD_DIM),
                           new_rows(dv, H, 2 * HEAD_DIM), page_table, lam_params, g_subln, l, Ts, lam_init)
        a_diff = o_d.reshape(Bs, H, Ts, 2 * HEAD_DIM).transpose(0, 2, 1, 3).reshape(Ms, DW).astype(BF16)

        kv5 = kv.reshape(Bs, Ts, 6, G_NSA, HEAD_DIM)
        new6 = [new_rows(kv[:, i * KVW:(i + 1) * KVW], G_NSA, HEAD_DIM) for i in range(6)]
        kcb, vcb = _compress_sample(cache_cmp_k, cache_cmp_v, page_table, w_cmp_k, w_cmp_v, l)
        qn = new_rows(nq, G_NSA, hg, HEAD_DIM).transpose(0, 2, 3, 1, 4).reshape(Bs, G_NSA * hg * SUBLANE, HEAD_DIM)
        gr = new_rows(sg, G_NSA, LANE)[..., :3 * hg].reshape(Bs, SUBLANE, G_NSA, hg, 3)
        gr = jnp.pad(gr.transpose(0, 2, 3, 1, 4).reshape(Bs, G_NSA * hg * SUBLANE, 3),
                     ((0, 0), (0, 0), (0, LANE - 3)))
        o_n = _nsa_sample(qn, kcb, vcb, cache_sel_k, cache_sel_v, new6[2], new6[3], new6[4], new6[5],
                          state_win_k, state_win_v, gr, page_table, l, Ts, hg)
        a_nsa = (o_n.reshape(Bs, G_NSA, hg, SUBLANE, HEAD_DIM)[:, :, :, :Ts].transpose(0, 3, 1, 2, 4)
                 .reshape(Ms, QW).astype(BF16))
        xs, hs = finish(xs, mod_s, a_conv, a_diff, a_nsa, smg)
        outs["s_dk"].append(dk.reshape(Bs, Ts, H, 2 * HEAD_DIM))
        outs["s_dv"].append(dv.reshape(Bs, Ts, H, 2 * HEAD_DIM))
        for i, n in enumerate(("s_ck", "s_cv", "s_sk", "s_sv")):
            outs[n].append(kv5[:, :, i])
        outs["s_wk"].append(jnp.concatenate([state_win_k[l], kv5[:, :, 4]], axis=1)[:, Ts:])
        outs["s_wv"].append(jnp.concatenate([state_win_v[l], kv5[:, :, 5]], axis=1)[:, Ts:])
        outs["s_conv"].append(u3)

    p_conv = _conv_states(outs.pop("p_conv"), cw - 1)
    s_conv = _conv_states(outs.pop("s_conv"), cw - 1, state_conv)
    st = {n: jnp.stack(v) for n, v in outs.items()}
    return (xp, xs.reshape(Bs, Ts, D),
            st["p_dk"], st["p_dv"], st["p_ck"], st["p_cv"], st["p_sk"], st["p_sv"], st["p_wk"], st["p_wv"],
            p_conv,
            st["s_dk"], st["s_dv"], st["s_ck"], st["s_cv"], st["s_sk"], st["s_sv"], st["s_wk"], st["s_wv"],
            s_conv)
```
